```python
import jax, jax.numpy as jnp
from jax import lax
import numpy as np

D_MODEL = 1024
BATCH = 16
SEQ = 4096
DEPTH = 2

CHUNK = 64
D_MIX = D_MODEL
EPS = 1e-6

RET_HEADS = 4
RET_DK = 64
RET_DV = 64
RET_WIDTH = RET_HEADS * RET_DV
ROPE_BASE = 10000.0

SSD_HEADS = 8
SSD_HEAD_DIM = 64
SSD_WIDTH = SSD_HEADS * SSD_HEAD_DIM
SSD_GROUPS = 2
SSD_HEADS_PER_GROUP = SSD_HEADS // SSD_GROUPS
SSD_STATE = 128
SSD_CONV = 4
SSD_CONV_DIM = SSD_WIDTH + 2 * SSD_GROUPS * SSD_STATE

LRU_WIDTH = D_MIX - RET_WIDTH - SSD_WIDTH
LRU_BLOCKS = 4
LRU_BLOCK_DIM = LRU_WIDTH // LRU_BLOCKS
LRU_CONV = 4
LRU_C = 8.0

D_FF = 2816
FFN_CONV = 3

PROJ_SIZES = (RET_HEADS * RET_DK, RET_HEADS * RET_DK, RET_WIDTH, RET_WIDTH,
              SSD_WIDTH, SSD_CONV_DIM, SSD_HEADS,
              LRU_WIDTH, LRU_WIDTH)
D_PROJ = sum(PROJ_SIZES)

kernel_name = 'hybrid_retention_ssd_rglru_convffn'


def rmsnorm(x, w):
    xf = x.astype(jnp.float32)
    y = xf * lax.rsqrt(jnp.mean(xf * xf, axis=-1, keepdims=True) + EPS)
    return (y * w.astype(jnp.float32)).astype(x.dtype)


def causal_dwconv(x, w, b):
    width = w.shape[0]
    y = lax.conv_general_dilated(x, w[:, None, :].astype(x.dtype), window_strides=(1,),
                                 padding=[(width - 1, 0)],
                                 dimension_numbers=('NWC', 'WIO', 'NWC'),
                                 feature_group_count=x.shape[-1])
    return y + b.astype(x.dtype)


def split_proj(p):
    pieces = []
    start = 0
    for size in PROJ_SIZES:
        pieces.append(p[..., start:start + size])
        start += size
    return pieces


def rotary(x, pos):
    half = x.shape[-1] // 2
    inv = ROPE_BASE ** (-jnp.arange(half, dtype=jnp.float32) / half)
    ang = pos[:, None] * inv[None, :]
    cos = jnp.cos(ang)[None, :, None, :]
    sin = jnp.sin(ang)[None, :, None, :]
    x1, x2 = x[..., :half], x[..., half:]
    return jnp.concatenate([x1 * cos - x2 * sin, x1 * sin + x2 * cos], axis=-1)


def retention(q, k, v, g):
    b, L = q.shape[0], q.shape[1]
    nc = L // CHUNK
    pos = jnp.arange(L, dtype=jnp.float32)
    q = rotary(q.reshape(b, L, RET_HEADS, RET_DK), pos) * (RET_DK ** -0.5)
    k = rotary(k.reshape(b, L, RET_HEADS, RET_DK), pos)
    qc = q.reshape(b, nc, CHUNK, RET_HEADS, RET_DK)
    kc = k.reshape(b, nc, CHUNK, RET_HEADS, RET_DK)
    vc = v.reshape(b, nc, CHUNK, RET_HEADS, RET_DV)
    log_gamma = jnp.log(1.0 - 2.0 ** (-5.0 - jnp.arange(RET_HEADS, dtype=jnp.float32)))
    idx = jnp.arange(CHUNK, dtype=jnp.float32)
    diff = idx[:, None] - idx[None, :]
    dmask = jnp.where(diff >= 0, jnp.exp(log_gamma[:, None, None] * jnp.maximum(diff, 0.0)), 0.0)
    scores = jnp.einsum('bclhd,bcshd->bhcls', qc, kc) * dmask[:, None]
    y_intra = jnp.einsum('bhcls,bcshe->bclhe', scores, vc)
    k_decay = jnp.exp(log_gamma[:, None] * (CHUNK - 1 - idx)[None, :])
    kv = jnp.einsum('bcshd,hs,bcshe->cbhde', kc, k_decay, vc)
    chunk_decay = jnp.exp(log_gamma * CHUNK)[:, None, None]

    def step(state, kv_c):
        return chunk_decay * state + kv_c, state

    _, s_prev = lax.scan(step, jnp.zeros(kv.shape[1:], kv.dtype), kv)
    q_decay = jnp.exp(log_gamma[:, None] * (idx + 1.0)[None, :])
    y_inter = jnp.einsum('bclhd,hl,cbhde->bclhe', qc, q_decay, s_prev)
    y = (y_intra + y_inter).reshape(b, L, RET_HEADS, RET_DV).astype(jnp.float32)
    mu = jnp.mean(y, axis=-1, keepdims=True)
    var = jnp.mean(jnp.square(y - mu), axis=-1, keepdims=True)
    y = ((y - mu) * lax.rsqrt(var + EPS)).reshape(b, L, RET_WIDTH)
    return y.astype(q.dtype) * jax.nn.silu(g)


def ssd(z, xbc, dt_raw, conv_w, conv_b, dt_bias, a_log, d_skip, norm_w):
    b, L = z.shape[0], z.shape[1]
    nc = L // CHUNK
    G, R, P, N = SSD_GROUPS, SSD_HEADS_PER_GROUP, SSD_HEAD_DIM, SSD_STATE
    xbc = jax.nn.silu(causal_dwconv(xbc, conv_w, conv_b))
    xs = xbc[..., :SSD_WIDTH]
    bm = xbc[..., SSD_WIDTH:SSD_WIDTH + G * N]
    cm = xbc[..., SSD_WIDTH + G * N:]
    dt = jax.nn.softplus(dt_raw.astype(jnp.float32) + dt_bias.astype(jnp.float32))
    a = -jnp.exp(a_log.astype(jnp.float32))
    xc = xs.reshape(b, nc, CHUNK, G, R, P)
    bc = bm.reshape(b, nc, CHUNK, G, N)
    cc = cm.reshape(b, nc, CHUNK, G, N)
    dtc = dt.reshape(b, nc, CHUNK, G, R)
    cum = jnp.cumsum((dtc * a.reshape(G, R)).transpose(0, 3, 4, 1, 2), axis=-1)
    tril = jnp.tril(jnp.ones((CHUNK, CHUNK), dtype=bool))
    seg = cum[..., :, None] - cum[..., None, :]
    lmat = jnp.where(tril, jnp.exp(jnp.where(tril, seg, 0.0)), 0.0)
    xdt = xc * dtc[..., None]
    cb = jnp.einsum('bclgn,bcsgn->bgcls', cc, bc)
    y_diag = jnp.einsum('bgcls,bgrcls,bcsgrp->bclgrp', cb, lmat, xdt)
    decay_states = jnp.exp(cum[..., -1:] - cum)
    states = jnp.einsum('bcsgn,bgrcs,bcsgrp->cbgrpn', bc, decay_states, xdt)
    chunk_decay = jnp.exp(cum[..., -1]).transpose(3, 0, 1, 2)

    def step(state, inp):
        dec, st = inp
        return dec[..., None, None] * state + st, state

    _, s_prev = lax.scan(step, jnp.zeros(states.shape[1:], states.dtype), (chunk_decay, states))
    y_off = jnp.einsum('bclgn,cbgrpn,bgrcl->bclgrp', cc, s_prev, jnp.exp(cum))
    y = y_diag + y_off + d_skip.reshape(G, R)[:, :, None] * xc
    y = y.reshape(b, L, SSD_WIDTH) * jax.nn.silu(z)
    yf = y.astype(jnp.float32).reshape(b, L, G, SSD_WIDTH // G)
    yf = yf * lax.rsqrt(jnp.mean(yf * yf, axis=-1, keepdims=True) + EPS)
    return (yf.reshape(b, L, SSD_WIDTH) * norm_w.astype(jnp.float32)).astype(z.dtype)


def rglru(gate, xin, conv_w, conv_b, w_a, b_a, w_x, b_x, lam):
    b, L = xin.shape[0], xin.shape[1]
    xc = causal_dwconv(xin, conv_w, conv_b)
    xb = xc.reshape(b, L, LRU_BLOCKS, LRU_BLOCK_DIM)
    r = jax.nn.sigmoid(jnp.einsum('blki,kij->blkj', xb, w_a).reshape(b, L, LRU_WIDTH) + b_a)
    i = jax.nn.sigmoid(jnp.einsum('blki,kij->blkj', xb, w_x).reshape(b, L, LRU_WIDTH) + b_x)
    log_a = -LRU_C * r.astype(jnp.float32) * jax.nn.softplus(-lam.astype(jnp.float32))
    a = jnp.exp(log_a)
    u = jnp.sqrt(-jnp.expm1(2.0 * log_a)) * (i * xc)

    def combine(lhs, rhs):
        a1, b1 = lhs
        a2, b2 = rhs
        return a1 * a2, a2 * b1 + b2

    _, h = lax.associative_scan(combine, (a, u), axis=1)
    return h.astype(xin.dtype) * jax.nn.gelu(gate)


def conv_ffn(x, w_up, conv_w, conv_b, w_down):
    h = causal_dwconv(x @ w_up, conv_w, conv_b)
    u, v = h[..., :D_FF], h[..., D_FF:]
    return (jax.nn.gelu(u) * v) @ w_down


def setup_inputs(seed: int = 0) -> dict:
    key = jax.random.key(seed)
    ks = jax.random.split(key, 24)
    f32 = jnp.float32

    def nrm(k, shape, scale):
        return jax.random.normal(k, shape, f32) * scale

    x = jax.random.normal(ks[0], (BATCH, SEQ, D_MODEL), f32)
    norm1_w = 1.0 + nrm(ks[1], (DEPTH, D_MODEL), 0.02)
    w_in = nrm(ks[2], (DEPTH, D_MODEL, D_PROJ), D_MODEL ** -0.5)
    ssd_conv_w = nrm(ks[3], (DEPTH, SSD_CONV, SSD_CONV_DIM), SSD_CONV ** -0.5)
    ssd_conv_b = nrm(ks[4], (DEPTH, SSD_CONV_DIM), 0.01)
    dt0 = jnp.exp(jax.random.uniform(ks[5], (DEPTH, SSD_HEADS), f32, np.log(1e-3), np.log(1e-1)))
    ssd_dt_bias = dt0 + jnp.log(-jnp.expm1(-dt0))
    ssd_a_log = jnp.log(jax.random.uniform(ks[6], (DEPTH, SSD_HEADS), f32, 1.0, 16.0))
    ssd_d = 1.0 + nrm(ks[7], (DEPTH, SSD_HEADS), 0.02)
    ssd_norm_w = 1.0 + nrm(ks[8], (DEPTH, SSD_WIDTH), 0.02)
    lru_conv_w = nrm(ks[9], (DEPTH, LRU_CONV, LRU_WIDTH), LRU_CONV ** -0.5)
    lru_conv_b = nrm(ks[10], (DEPTH, LRU_WIDTH), 0.01)
    lru_w_a = nrm(ks[11], (DEPTH, LRU_BLOCKS, LRU_BLOCK_DIM, LRU_BLOCK_DIM), LRU_BLOCK_DIM ** -0.5)
    lru_b_a = nrm(ks[12], (DEPTH, LRU_WIDTH), 0.01)
    lru_w_x = nrm(ks[13], (DEPTH, LRU_BLOCKS, LRU_BLOCK_DIM, LRU_BLOCK_DIM), LRU_BLOCK_DIM ** -0.5)
    lru_b_x = nrm(ks[14], (DEPTH, LRU_WIDTH), 0.01)
    a0 = jax.random.uniform(ks[15], (DEPTH, LRU_WIDTH), f32, 0.9, 0.999) ** (1.0 / LRU_C)
    lru_lambda = jnp.log(a0) - jnp.log1p(-a0)
    w_out = nrm(ks[16], (DEPTH, D_MIX, D_MODEL), D_MIX ** -0.5)
    norm2_w = 1.0 + nrm(ks[17], (DEPTH, D_MODEL), 0.02)
    ffn_w_up = nrm(ks[18], (DEPTH, D_MODEL, 2 * D_FF), D_MODEL ** -0.5)
    ffn_conv_w = nrm(ks[19], (DEPTH, FFN_CONV, 2 * D_FF), FFN_CONV ** -0.5)
    ffn_conv_b = nrm(ks[20], (DEPTH, 2 * D_FF), 0.01)
    ffn_w_down = nrm(ks[21], (DEPTH, D_FF, D_MODEL), D_FF ** -0.5)
    final_norm_w = 1.0 + nrm(ks[22], (D_MODEL,), 0.02)
    return {'x': x, 'norm1_w': norm1_w, 'w_in': w_in,
            'ssd_conv_w': ssd_conv_w, 'ssd_conv_b': ssd_conv_b, 'ssd_dt_bias': ssd_dt_bias,
            'ssd_a_log': ssd_a_log, 'ssd_d': ssd_d, 'ssd_norm_w': ssd_norm_w,
            'lru_conv_w': lru_conv_w, 'lru_conv_b': lru_conv_b, 'lru_w_a': lru_w_a, 'lru_b_a': lru_b_a,
            'lru_w_x': lru_w_x, 'lru_b_x': lru_b_x, 'lru_lambda': lru_lambda,
            'w_out': w_out, 'norm2_w': norm2_w, 'ffn_w_up': ffn_w_up, 'ffn_conv_w': ffn_conv_w,
            'ffn_conv_b': ffn_conv_b, 'ffn_w_down': ffn_w_down, 'final_norm_w': final_norm_w}


def reference(x, norm1_w, w_in, ssd_conv_w, ssd_conv_b, ssd_dt_bias, ssd_a_log, ssd_d, ssd_norm_w,
              lru_conv_w, lru_conv_b, lru_w_a, lru_b_a, lru_w_x, lru_b_x, lru_lambda,
              w_out, norm2_w, ffn_w_up, ffn_conv_w, ffn_conv_b, ffn_w_down, final_norm_w):
    for l in range(DEPTH):
        h = rmsnorm(x, norm1_w[l])
        q, k, v, g, z, xbc, dt_raw, lru_gate, lru_x = split_proj(h @ w_in[l])
        y_ret = retention(q, k, v, g)
        y_ssd = ssd(z, xbc, dt_raw, ssd_conv_w[l], ssd_conv_b[l], ssd_dt_bias[l], ssd_a_log[l],
                    ssd_d[l], ssd_norm_w[l])
        y_lru = rglru(lru_gate, lru_x, lru_conv_w[l], lru_conv_b[l], lru_w_a[l], lru_b_a[l],
                      lru_w_x[l], lru_b_x[l], lru_lambda[l])
        y = jnp.concatenate([y_ret, y_ssd, y_lru], axis=-1)
        x = x + (y @ w_out[l]).astype(x.dtype)
        h = rmsnorm(x, norm2_w[l])
        x = x + conv_ffn(h, ffn_w_up[l], ffn_conv_w[l], ffn_conv_b[l], ffn_w_down[l]).astype(x.dtype)
    return rmsnorm(x, final_norm_w)
```

```python
import functools
import math

import jax
import jax.numpy as jnp
import numpy as np
from jax import lax
from jax.experimental import pallas as pl
from jax.experimental.pallas import tpu as pltpu

D_MODEL = 1024
CHUNK = 64
EPS = 1e-6

RET_HEADS = 4
RET_DK = 64
RET_WIDTH = 256
ROPE_BASE = 10000.0

SSD_HEADS = 8
SSD_WIDTH = 512
SSD_GROUPS = 2
SSD_STATE = 128
SSD_CONV = 4
SSD_CONV_DIM = 1024
SSD_GROUP_WIDTH = SSD_WIDTH // SSD_GROUPS

LRU_WIDTH = 256
LRU_BLOCKS = 4
LRU_CONV = 4
LRU_C = 8.0

D_FF = 2816
FFN_CONV = 3

SUBLANES = 8
VMEM_LIMIT_BYTES = 56 * 1024 * 1024

OFF_Q = 0
OFF_K = 256
OFF_V = 512
OFF_G = 768
OFF_Z = 1024
OFF_XBC = 1536
OFF_DT = 2560
OFF_LG = 3072
OFF_LX = 3328
D_PROJ_PAD = 3584

TOKENS_MIX = 256
TOKENS_FFN = 256
FFN_COLS = 256

F32 = jnp.float32
BF16 = jnp.bfloat16


def _dot(a, b):
    return jnp.dot(a, b, preferred_element_type=F32)


def _dot_nt(a, b):
    return lax.dot_general(a, b, (((1,), (1,)), ((), ())), preferred_element_type=F32)


def _dot_tn(a, b):
    return lax.dot_general(a, b, (((0,), (0,)), ((), ())), preferred_element_type=F32)


def _sigmoid(x):
    return 1.0 / (1.0 + jnp.exp(-x))


def _silu(x):
    return x * _sigmoid(x)


def _softplus(x):
    return jnp.maximum(x, 0.0) + jnp.log1p(jnp.exp(-jnp.abs(x)))


def _gelu_tanh(x):
    c = math.sqrt(2.0 / math.pi)
    return 0.5 * x * (1.0 + jnp.tanh(c * (x + 0.044715 * (x * x * x))))


def _rmsnorm(x, w):
    ms = jnp.mean(x * x, axis=-1, keepdims=True)
    return x * lax.rsqrt(ms + EPS) * w


def _split3_bf16(x):
    hi = x.astype(BF16)
    r1 = x - hi.astype(F32)
    mid = r1.astype(BF16)
    lo = (r1 - mid.astype(F32)).astype(BF16)
    return hi, mid, lo


def _causal_conv(buf_ref, w_ref, b_ref, width, t, cols=None):
    cs = slice(None) if cols is None else cols
    acc = None
    for j in range(width):
        start = SUBLANES - (width - 1) + j
        term = buf_ref[pl.ds(start, t), cs] * w_ref[j:j + 1, cs]
        acc = term if acc is None else acc + term
    return acc + b_ref[0:1, cs]


def _mixer_kernel(x_ref, cos_ref, sin_ref, n1_ref, win_ref,
                  dmask_ref, qdec_ref, kdec_ref, cdec_ref,
                  scw_ref, scb_ref, dtb_ref, alog_ref, dsk_ref, snw_ref,
                  lcw_ref, lcb_ref, lw_ref, lb_ref, lam_ref, wout_ref,
                  out_ref,
                  q_s, k_s, v_s, cum_s, xdt_s, b_s, c_s, yret_s, yssd_s,
                  sret_s, sssd_s, hlru_s, xbc_buf, lx_buf):
    t = x_ref.shape[0]
    n_chunks = t // CHUNK
    j = pl.program_id(1)

    @pl.when(j == 0)
    def _():
        sret_s[...] = jnp.zeros_like(sret_s)
        sssd_s[...] = jnp.zeros_like(sssd_s)
        hlru_s[...] = jnp.zeros_like(hlru_s)
        xbc_buf[0:SUBLANES, :] = jnp.zeros((SUBLANES, SSD_CONV_DIM), F32)
        lx_buf[0:SUBLANES, :] = jnp.zeros((SUBLANES, LRU_WIDTH), F32)

    x = x_ref[...]
    h = _rmsnorm(x, n1_ref[...]).astype(BF16)

    def proj(off, width):
        return _dot(h, win_ref[:, off:off + width])

    cos = cos_ref[...]
    sin = sin_ref[...]

    def rope(p):
        p1, p2 = p[:, :128], p[:, 128:]
        return jnp.concatenate([p1 * cos - p2 * sin, p1 * sin + p2 * cos], axis=-1)

    q_s[...] = rope(proj(OFF_Q, 256))
    k_s[...] = rope(proj(OFF_K, 256))
    v_s[...] = proj(OFF_V, 256)

    xbc_buf[pl.ds(SUBLANES, t), :] = proj(OFF_XBC, SSD_CONV_DIM)
    xbc = _silu(_causal_conv(xbc_buf, scw_ref, scb_ref, SSD_CONV, t))
    xbc_buf[0:SUBLANES, :] = xbc_buf[pl.ds(t, SUBLANES), :]
    xs = xbc[:, :SSD_WIDTH]
    b_s[...] = xbc[:, SSD_WIDTH:SSD_WIDTH + SSD_GROUPS * SSD_STATE]
    c_s[...] = xbc[:, SSD_WIDTH + SSD_GROUPS * SSD_STATE:]
    dt = _softplus(proj(OFF_DT, SSD_WIDTH) + dtb_ref[...])
    da = dt * (-jnp.exp(alog_ref[...]))
    xdt_s[...] = xs * dt
    ri = lax.broadcasted_iota(jnp.int32, (t, t), 0)
    ci = lax.broadcasted_iota(jnp.int32, (t, t), 1)
    tril = jnp.where((ri >= ci) & (ri // CHUNK == ci // CHUNK), 1.0, 0.0).astype(BF16)
    d_hi, d_mid, d_lo = _split3_bf16(da)
    cum_s[...] = _dot(tril, d_hi) + _dot(tril, d_mid) + _dot(tril, d_lo)
    dskip_xs = xs * dsk_ref[...]

    r4 = lax.broadcasted_iota(jnp.int32, (4 * CHUNK, 256), 0) // CHUNK
    l4 = lax.broadcasted_iota(jnp.int32, (4 * CHUNK, 256), 1)
    mask_qk = r4 == (l4 % 128) // 32
    mask_bd = r4 == l4 // CHUNK
    rc = lax.broadcasted_iota(jnp.int32, (CHUNK, 256), 0)
    lc = lax.broadcasted_iota(jnp.int32, (CHUNK, 256), 1) % CHUNK
    mask_tri = rc >= lc
    mask_diag = rc == lc
    rs = lax.broadcasted_iota(jnp.int32, (256, 256), 0)
    ls = lax.broadcasted_iota(jnp.int32, (256, 256), 1)
    mask_state = (rs % 128) // 32 == ls // CHUNK

    def tile4(a):
        return jnp.concatenate([a, a, a, a], axis=0)

    def chunk_body(c, carry):
        rows = pl.ds(pl.multiple_of(c * CHUNK, CHUNK), CHUNK)
        qc = q_s[rows, :]
        kc = k_s[rows, :]
        vc = v_s[rows, :]
        kt4 = jnp.where(mask_qk, tile4(kc), 0.0).astype(BF16)
        scores = _dot_nt(qc.astype(BF16), kt4) * dmask_ref[...]
        vbd = jnp.where(mask_bd, tile4(vc), 0.0).astype(BF16)
        s_prev = sret_s[...]
        y = _dot(scores.astype(BF16), vbd)
        y = y + _dot((qc * qdec_ref[...]).astype(BF16), s_prev.astype(BF16))
        yret_s[rows, :] = y
        kv = _dot_tn((kc * kdec_ref[...]).astype(BF16), vc.astype(BF16))
        sret_s[...] = s_prev * cdec_ref[...] + jnp.where(mask_state, kv, 0.0)
        for g in range(SSD_GROUPS):
            lanes = slice(g * SSD_GROUP_WIDTH, (g + 1) * SSD_GROUP_WIDTH)
            nl = slice(g * SSD_STATE, (g + 1) * SSD_STATE)
            cumc = cum_s[rows, lanes]
            xdtc = xdt_s[rows, lanes]
            bc = b_s[rows, nl]
            cc = c_s[rows, nl].astype(BF16)
            cum_row = jnp.sum(jnp.where(mask_diag, cumc, 0.0), axis=0, keepdims=True)
            lmat = jnp.where(mask_tri, jnp.exp(cumc - cum_row), 0.0)
            cb = _dot_nt(cc, tile4(bc).astype(BF16))
            xbd = jnp.where(mask_bd, tile4(xdtc), 0.0).astype(BF16)
            y_diag = _dot((cb * lmat).astype(BF16), xbd)
            st_prev = sssd_s[g]
            y_off = jnp.exp(cumc) * _dot(cc, st_prev.astype(BF16))
            yssd_s[rows, lanes] = y_diag + y_off
            cum_last = cumc[CHUNK - 1:CHUNK, :]
            upd = _dot_tn(bc.astype(BF16), (xdtc * jnp.exp(cum_last - cumc)).astype(BF16))
            sssd_s[g] = st_prev * jnp.exp(cum_last) + upd
        return carry

    lax.fori_loop(0, n_chunks, chunk_body, 0)

    yr = yret_s[...]
    lane = lax.broadcasted_iota(jnp.int32, (t, RET_WIDTH), 1) // RET_DK
    mu = jnp.zeros_like(yr)
    for hd in range(RET_HEADS):
        m = lane == hd
        s = jnp.sum(jnp.where(m, yr, 0.0), axis=-1, keepdims=True) * (1.0 / RET_DK)
        mu = jnp.where(m, s, mu)
    yc = yr - mu
    var = jnp.zeros_like(yr)
    for hd in range(RET_HEADS):
        m = lane == hd
        s = jnp.sum(jnp.where(m, yc * yc, 0.0), axis=-1, keepdims=True) * (1.0 / RET_DK)
        var = jnp.where(m, s, var)
    y_ret = yc * lax.rsqrt(var + EPS) * _silu(proj(OFF_G, RET_WIDTH))

    ys = (yssd_s[...] + dskip_xs) * _silu(proj(OFF_Z, SSD_WIDTH))
    parts = []
    for g in range(SSD_GROUPS):
        yg = ys[:, g * SSD_GROUP_WIDTH:(g + 1) * SSD_GROUP_WIDTH]
        parts.append(yg * lax.rsqrt(jnp.mean(yg * yg, axis=-1, keepdims=True) + EPS))
    y_ssd = jnp.concatenate(parts, axis=-1) * snw_ref[...]

    lx_buf[pl.ds(SUBLANES, t), :] = proj(OFF_LX, LRU_WIDTH)
    xc = _causal_conv(lx_buf, lcw_ref, lcb_ref, LRU_CONV, t)
    lx_buf[0:SUBLANES, :] = lx_buf[pl.ds(t, SUBLANES), :]
    gates = _dot(xc.astype(BF16), lw_ref[...]) + lb_ref[...]
    r_gate = _sigmoid(gates[:, :LRU_WIDTH])
    i_gate = _sigmoid(gates[:, LRU_WIDTH:])
    log_a = (-LRU_C) * r_gate * _softplus(-lam_ref[...])
    a = jnp.exp(log_a)
    u = jnp.sqrt(-jnp.tanh(log_a) * (1.0 + a * a)) * (i_gate * xc)
    row = lax.broadcasted_iota(jnp.int32, (t, LRU_WIDTH), 0)
    u = u + jnp.where(row == 0, a * hlru_s[0:1, :], 0.0)
    step = 1
    while step < t:
        keep = row >= step
        a_sh = jnp.where(keep, pltpu.roll(a, step, axis=0), 1.0)
        u_sh = jnp.where(keep, pltpu.roll(u, step, axis=0), 0.0)
        u = a * u_sh + u
        a = a * a_sh
        step *= 2
    hlru_s[...] = u[t - SUBLANES:t, :][SUBLANES - 1:SUBLANES, :] * jnp.ones((SUBLANES, 1), F32)
    y_lru = u * _gelu_tanh(proj(OFF_LG, LRU_WIDTH))

    o = _dot(y_ret.astype(BF16), wout_ref[0:256, :])
    o = o + _dot(y_ssd.astype(BF16), wout_ref[256:768, :])
    o = o + _dot(y_lru.astype(BF16), wout_ref[768:1024, :])
    out_ref[...] = x + o


def _const_spec(shape):
    nd = len(shape)
    return pl.BlockSpec(shape, lambda b, j: (0,) * nd)


def _mixer_call(x, tables, p):
    bsz, seq, d = x.shape
    t = TOKENS_MIX
    cos_t, sin_t, dmask_t, qdec_t, kdec_t, cdec_t = tables
    xspec = pl.BlockSpec((None, t, d), lambda b, j: (b, j, 0))
    tspec = pl.BlockSpec((t, 128), lambda b, j: (j, 0))
    consts = [p['n1'], p['win'], dmask_t, qdec_t, kdec_t, cdec_t,
              p['scw'], p['scb'], p['dtb'], p['alog'], p['dsk'], p['snw'],
              p['lcw'], p['lcb'], p['lw'], p['lb'], p['lam'], p['wout']]
    scratch = [
        pltpu.VMEM((t, 256), F32), pltpu.VMEM((t, 256), F32), pltpu.VMEM((t, 256), F32),
        pltpu.VMEM((t, SSD_WIDTH), F32), pltpu.VMEM((t, SSD_WIDTH), F32),
        pltpu.VMEM((t, 256), F32), pltpu.VMEM((t, 256), F32),
        pltpu.VMEM((t, RET_WIDTH), F32), pltpu.VMEM((t, SSD_WIDTH), F32),
        pltpu.VMEM((256, 256), F32), pltpu.VMEM((SSD_GROUPS, SSD_STATE, SSD_GROUP_WIDTH), F32),
        pltpu.VMEM((SUBLANES, LRU_WIDTH), F32),
        pltpu.VMEM((t + SUBLANES, SSD_CONV_DIM), F32), pltpu.VMEM((t + SUBLANES, LRU_WIDTH), F32),
    ]
    return pl.pallas_call(
        _mixer_kernel,
        grid=(bsz, seq // t),
        in_specs=[xspec, tspec, tspec] + [_const_spec(c.shape) for c in consts],
        out_specs=xspec,
        out_shape=jax.ShapeDtypeStruct(x.shape, x.dtype),
        scratch_shapes=scratch,
        compiler_params=pltpu.CompilerParams(
            dimension_semantics=("parallel", "arbitrary"),
            vmem_limit_bytes=VMEM_LIMIT_BYTES),
        name="mixer",
    )(x, cos_t, sin_t, *consts)


def _ffn_kernel(x_ref, n2_ref, wup_ref, cw_ref, cb_ref, wdn_ref, fin_ref, out_ref, up_buf,
                *, final_norm):
    t = x_ref.shape[0]
    j = pl.program_id(1)

    @pl.when(j == 0)
    def _():
        up_buf[0:SUBLANES, :] = jnp.zeros((SUBLANES, 2 * D_FF), F32)

    x = x_ref[...]
    h = _rmsnorm(x, n2_ref[...]).astype(BF16)
    acc = x
    for cidx in range(D_FF // FFN_COLS):
        halves = []
        for base in (0, D_FF):
            cols = slice(base + cidx * FFN_COLS, base + (cidx + 1) * FFN_COLS)
            up_buf[pl.ds(SUBLANES, t), cols] = _dot(h, wup_ref[:, cols])
            halves.append(_causal_conv(up_buf, cw_ref, cb_ref, FFN_CONV, t, cols))
            up_buf[0:SUBLANES, cols] = up_buf[pl.ds(t, SUBLANES), cols]
        act = (_gelu_tanh(halves[0]) * halves[1]).astype(BF16)
        acc = acc + _dot(act, wdn_ref[cidx * FFN_COLS:(cidx + 1) * FFN_COLS, :])
    if final_norm:
        acc = _rmsnorm(acc, fin_ref[...])
    out_ref[...] = acc


def _ffn_call(x, p, final_w, final_norm):
    bsz, seq, d = x.shape
    t = TOKENS_FFN
    xspec = pl.BlockSpec((None, t, d), lambda b, j: (b, j, 0))
    consts = [p['n2'], p['wup'], p['fcw'], p['fcb'], p['wdn'], final_w]
    return pl.pallas_call(
        functools.partial(_ffn_kernel, final_norm=final_norm),
        grid=(bsz, seq // t),
        in_specs=[xspec] + [_const_spec(c.shape) for c in consts],
        out_specs=xspec,
        out_shape=jax.ShapeDtypeStruct(x.shape, x.dtype),
        scratch_shapes=[pltpu.VMEM((t + SUBLANES, 2 * D_FF), F32)],
        compiler_params=pltpu.CompilerParams(
            dimension_semantics=("parallel", "arbitrary"),
            vmem_limit_bytes=VMEM_LIMIT_BYTES),
        name="ffn",
    )(x, *consts)


def _retention_tables(seq):
    half = RET_DK // 2
    pos = jnp.arange(seq, dtype=F32)
    inv = ROPE_BASE ** (-jnp.arange(half, dtype=F32) / half)
    ang = pos[:, None] * inv[None, :]
    cos_t = jnp.tile(jnp.cos(ang), (1, RET_HEADS))
    sin_t = jnp.tile(jnp.sin(ang), (1, RET_HEADS))
    log_gamma = jnp.log(1.0 - 2.0 ** (-5.0 - jnp.arange(RET_HEADS, dtype=F32)))
    idx = jnp.arange(CHUNK, dtype=F32)
    diff = idx[:, None] - idx[None, :]
    dmask = jnp.where(diff >= 0, jnp.exp(log_gamma[:, None, None] * jnp.maximum(diff, 0.0)), 0.0)
    scale = RET_DK ** -0.5
    dmask_t = jnp.transpose(dmask, (1, 0, 2)).reshape(CHUNK, RET_HEADS * CHUNK) * scale
    qk_head = (np.arange(256) % 128) // 32
    q_decay = jnp.exp(log_gamma[:, None] * (idx + 1.0)[None, :])
    k_decay = jnp.exp(log_gamma[:, None] * (CHUNK - 1 - idx)[None, :])
    qdec_t = q_decay.T[:, qk_head] * scale
    kdec_t = k_decay.T[:, qk_head]
    cdec_t = jnp.exp(log_gamma * CHUNK)[np.arange(256) // RET_DK][None, :]
    return cos_t, sin_t, dmask_t, qdec_t, kdec_t, cdec_t


def _layer_params(l, norm1_w, w_in, ssd_conv_w, ssd_conv_b, ssd_dt_bias, ssd_a_log, ssd_d,
                  ssd_norm_w, lru_conv_w, lru_conv_b, lru_w_a, lru_b_a, lru_w_x, lru_b_x,
                  lru_lambda, w_out, norm2_w, ffn_w_up, ffn_conv_w, ffn_conv_b, ffn_w_down):
    w = w_in[l]
    j = np.arange(256)
    qk_perm = ((j % 128) // 32) * RET_DK + (j // 128) * (RET_DK // 2) + j % 32
    head_of_lane = np.arange(SSD_WIDTH) // (SSD_WIDTH // SSD_HEADS)
    o = 0
    wq = w[:, o:o + 256][:, qk_perm]; o += 256
    wk = w[:, o:o + 256][:, qk_perm]; o += 256
    wv = w[:, o:o + 256]; o += 256
    wg = w[:, o:o + 256]; o += 256
    wz = w[:, o:o + SSD_WIDTH]; o += SSD_WIDTH
    wxbc = w[:, o:o + SSD_CONV_DIM]; o += SSD_CONV_DIM
    wdt = w[:, o:o + SSD_HEADS][:, head_of_lane]; o += SSD_HEADS
    wlg = w[:, o:o + LRU_WIDTH]; o += LRU_WIDTH
    wlx = w[:, o:o + LRU_WIDTH]
    win = jnp.concatenate([wq, wk, wv, wg, wz, wxbc, wdt, wlg, wlx], axis=1).astype(BF16)
    blk = LRU_WIDTH // LRU_BLOCKS

    def block_diag(wb):
        full = jnp.zeros((LRU_WIDTH, LRU_WIDTH), F32)
        for kb in range(LRU_BLOCKS):
            full = full.at[kb * blk:(kb + 1) * blk, kb * blk:(kb + 1) * blk].set(wb[kb])
        return full

    lw = jnp.concatenate([block_diag(lru_w_a[l]), block_diag(lru_w_x[l])], axis=1).astype(BF16)
    lb = jnp.concatenate([lru_b_a[l], lru_b_x[l]])[None, :]
    return dict(
        n1=norm1_w[l][None, :], win=win,
        scw=ssd_conv_w[l], scb=ssd_conv_b[l][None, :],
        dtb=ssd_dt_bias[l][head_of_lane][None, :], alog=ssd_a_log[l][head_of_lane][None, :],
        dsk=ssd_d[l][head_of_lane][None, :], snw=ssd_norm_w[l][None, :],
        lcw=lru_conv_w[l], lcb=lru_conv_b[l][None, :], lw=lw, lb=lb,
        lam=lru_lambda[l][None, :], wout=w_out[l].astype(BF16),
        n2=norm2_w[l][None, :], wup=ffn_w_up[l].astype(BF16), fcw=ffn_conv_w[l],
        fcb=ffn_conv_b[l][None, :], wdn=ffn_w_down[l].astype(BF16))


def kernel(x, norm1_w, w_in, ssd_conv_w, ssd_conv_b, ssd_dt_bias, ssd_a_log, ssd_d, ssd_norm_w, lru_conv_w, lru_conv_b, lru_w_a, lru_b_a, lru_w_x, lru_b_x, lru_lambda, w_out, norm2_w, ffn_w_up, ffn_conv_w, ffn_conv_b, ffn_w_down, final_norm_w):
    depth = w_in.shape[0]
    seq = x.shape[1]
    assert seq % TOKENS_MIX == 0 and seq % TOKENS_FFN == 0 and x.shape[2] == D_MODEL
    tables = _retention_tables(seq)
    fin = final_norm_w[None, :]
    for l in range(depth):
        p = _layer_params(l, norm1_w, w_in, ssd_conv_w, ssd_conv_b, ssd_dt_bias, ssd_a_log, ssd_d,
                          ssd_norm_w, lru_conv_w, lru_conv_b, lru_w_a, lru_b_a, lru_w_x, lru_b_x,
                          lru_lambda, w_out, norm2_w, ffn_w_up, ffn_conv_w, ffn_conv_b, ffn_w_down)
        x = _mixer_call(x, tables, p)
        x = _ffn_call(x, p, fin, final_norm=(l == depth - 1))
    return x
```

```python
import functools
import math

import jax
import jax.numpy as jnp
import numpy as np
from jax import lax
from jax.experimental import pallas as pl
from jax.experimental.pallas import tpu as pltpu

D_MODEL = 1024
CHUNK = 64
EPS = 1e-6

RET_HEADS = 4
RET_DK = 64
RET_WIDTH = 256
ROPE_BASE = 10000.0

SSD_HEADS = 8
SSD_WIDTH = 512
SSD_GROUPS = 2
SSD_STATE = 128
SSD_CONV = 4
SSD_CONV_DIM = 1024
SSD_GROUP_WIDTH = SSD_WIDTH // SSD_GROUPS

LRU_WIDTH = 256
LRU_BLOCKS = 4
LRU_CONV = 4
LRU_C = 8.0

D_FF = 2816
FFN_CONV = 3

SUBLANES = 8
VMEM_LIMIT_BYTES = 56 * 1024 * 1024

OFF_Q = 0
OFF_K = 256
OFF_V = 512
OFF_G = 768
OFF_Z = 1024
OFF_XBC = 1536
OFF_DT = 2560
OFF_LG = 3072
OFF_LX = 3328
D_PROJ_PAD = 3584

TOKENS_MIX = 256
TOKENS_FFN = 256
FFN_COLS = 256
FFN_LOOKAHEAD = 2

F32 = jnp.float32
BF16 = jnp.bfloat16


def _dot(a, b):
    return jnp.dot(a, b, preferred_element_type=F32)


def _dot_nt(a, b):
    return lax.dot_general(a, b, (((1,), (1,)), ((), ())), preferred_element_type=F32)


def _dot_tn(a, b):
    return lax.dot_general(a, b, (((0,), (0,)), ((), ())), preferred_element_type=F32)


def _sigmoid(x):
    return 1.0 / (1.0 + jnp.exp(-x))


def _silu(x):
    return x * _sigmoid(x)


def _softplus(x):
    return jnp.maximum(x, 0.0) + jnp.log(1.0 + jnp.exp(-jnp.abs(x)))


def _gelu_tanh(x):
    c = math.sqrt(2.0 / math.pi)
    return 0.5 * x * (1.0 + jnp.tanh(c * (x + 0.044715 * (x * x * x))))


def _rmsnorm(x, w):
    ms = jnp.mean(x * x, axis=-1, keepdims=True)
    return x * lax.rsqrt(ms + EPS) * w


def _split3_bf16(x):
    hi = x.astype(BF16)
    r1 = x - hi.astype(F32)
    mid = r1.astype(BF16)
    lo = (r1 - mid.astype(F32)).astype(BF16)
    return hi, mid, lo


def _causal_conv(cur, tail_ref, w_ref, b_ref, width, cols=None):
    cs = slice(None) if cols is None else cols
    t = cur.shape[0]
    row = lax.broadcasted_iota(jnp.int32, cur.shape, 0)
    prev = tail_ref[:, cs]
    acc = cur * w_ref[width - 1:width, cs] + b_ref[0:1, cs]
    for back in range(1, width):
        shifted = pltpu.roll(cur, back, axis=0)
        head = pltpu.roll(prev, back, axis=0)
        head = jnp.concatenate([head, jnp.zeros((t - SUBLANES, cur.shape[1]), F32)], axis=0)
        shifted = jnp.where(row < back, head, shifted)
        acc = acc + shifted * w_ref[width - 1 - back:width - back, cs]
    tail_ref[:, cs] = cur[t - SUBLANES:, :]
    return acc


def _mixer_kernel(x_ref, cos_ref, sin_ref, n1_ref, win_ref,
                  dmask_ref, qdec_ref, kdec_ref, cdec_ref,
                  scw_ref, scb_ref, dtb_ref, alog_ref, dsk_ref, snw_ref,
                  lcw_ref, lcb_ref, lw_ref, lb_ref, lam_ref, wout_ref,
                  out_ref,
                  q_s, k_s, v_s, cum_s, xdt_s, b_s, c_s, yret_s, yssd_s,
                  sret_s, sssd_s, hlru_s, xbc_buf, lx_buf):
    t = x_ref.shape[0]
    n_chunks = t // CHUNK
    j = pl.program_id(1)

    @pl.when(j == 0)
    def _():
        sret_s[...] = jnp.zeros_like(sret_s)
        sssd_s[...] = jnp.zeros_like(sssd_s)
        hlru_s[...] = jnp.zeros_like(hlru_s)
        xbc_buf[...] = jnp.zeros_like(xbc_buf)
        lx_buf[...] = jnp.zeros_like(lx_buf)

    x = x_ref[...]
    h = _rmsnorm(x, n1_ref[...]).astype(BF16)

    def proj(off, width):
        return _dot(h, win_ref[:, off:off + width])

    cos = cos_ref[...]
    sin = sin_ref[...]

    def rope(p):
        p1, p2 = p[:, :128], p[:, 128:]
        return jnp.concatenate([p1 * cos - p2 * sin, p1 * sin + p2 * cos], axis=-1)

    q_s[...] = rope(proj(OFF_Q, 256))
    k_s[...] = rope(proj(OFF_K, 256))
    v_s[...] = proj(OFF_V, 256)

    xbc = _silu(_causal_conv(proj(OFF_XBC, SSD_CONV_DIM), xbc_buf, scw_ref, scb_ref, SSD_CONV))
    xs = xbc[:, :SSD_WIDTH]
    b_s[...] = xbc[:, SSD_WIDTH:SSD_WIDTH + SSD_GROUPS * SSD_STATE]
    c_s[...] = xbc[:, SSD_WIDTH + SSD_GROUPS * SSD_STATE:]
    dt = _softplus(proj(OFF_DT, SSD_WIDTH) + dtb_ref[...])
    da = dt * (-jnp.exp(alog_ref[...]))
    xdt_s[...] = xs * dt
    ri = lax.broadcasted_iota(jnp.int32, (t, t), 0)
    ci = lax.broadcasted_iota(jnp.int32, (t, t), 1)
    tril = jnp.where((ri >= ci) & (ri // CHUNK == ci // CHUNK), 1.0, 0.0).astype(BF16)
    d_hi, d_mid, d_lo = _split3_bf16(da)
    cum_s[...] = _dot(tril, d_hi) + _dot(tril, d_mid) + _dot(tril, d_lo)
    dskip_xs = xs * dsk_ref[...]

    r4 = lax.broadcasted_iota(jnp.int32, (4 * CHUNK, 256), 0) // CHUNK
    l4 = lax.broadcasted_iota(jnp.int32, (4 * CHUNK, 256), 1)
    mask_qk = r4 == (l4 % 128) // 32
    mask_bd = r4 == l4 // CHUNK
    rc = lax.broadcasted_iota(jnp.int32, (CHUNK, 256), 0)
    lc = lax.broadcasted_iota(jnp.int32, (CHUNK, 256), 1) % CHUNK
    mask_tri = rc >= lc
    mask_diag = rc == lc
    rs = lax.broadcasted_iota(jnp.int32, (256, 256), 0)
    ls = lax.broadcasted_iota(jnp.int32, (256, 256), 1)
    mask_state = (rs % 128) // 32 == ls // CHUNK

    def tile4(a):
        return jnp.concatenate([a, a, a, a], axis=0)

    def chunk_body(c, carry):
        rows = pl.ds(pl.multiple_of(c * CHUNK, CHUNK), CHUNK)
        qc = q_s[rows, :]
        kc = k_s[rows, :]
        vc = v_s[rows, :]
        kt4 = jnp.where(mask_qk, tile4(kc), 0.0).astype(BF16)
        scores = _dot_nt(qc.astype(BF16), kt4) * dmask_ref[...]
        vbd = jnp.where(mask_bd, tile4(vc), 0.0).astype(BF16)
        s_prev = sret_s[...]
        y = _dot(scores.astype(BF16), vbd)
        y = y + _dot((qc * qdec_ref[...]).astype(BF16), s_prev.astype(BF16))
        yret_s[rows, :] = y
        kv = _dot_tn((kc * kdec_ref[...]).astype(BF16), vc.astype(BF16))
        sret_s[...] = s_prev * cdec_ref[...] + jnp.where(mask_state, kv, 0.0)
        for g in range(SSD_GROUPS):
            lanes = slice(g * SSD_GROUP_WIDTH, (g + 1) * SSD_GROUP_WIDTH)
            nl = slice(g * SSD_STATE, (g + 1) * SSD_STATE)
            cumc = cum_s[rows, lanes]
            xdtc = xdt_s[rows, lanes]
            bc = b_s[rows, nl]
            cc = c_s[rows, nl].astype(BF16)
            cum_row = jnp.sum(jnp.where(mask_diag, cumc, 0.0), axis=0, keepdims=True)
            lmat = jnp.where(mask_tri, jnp.exp(cumc - cum_row), 0.0)
            cb = _dot_nt(cc, tile4(bc).astype(BF16))
            xbd = jnp.where(mask_bd, tile4(xdtc), 0.0).astype(BF16)
            y_diag = _dot((cb * lmat).astype(BF16), xbd)
            st_prev = sssd_s[g]
            y_off = jnp.exp(cumc) * _dot(cc, st_prev.astype(BF16))
            yssd_s[rows, lanes] = y_diag + y_off
            cum_last = cumc[CHUNK - 1:CHUNK, :]
            upd = _dot_tn(bc.astype(BF16), (xdtc * jnp.exp(cum_last - cumc)).astype(BF16))
            sssd_s[g] = st_prev * jnp.exp(cum_last) + upd
        return carry

    lax.fori_loop(0, n_chunks, chunk_body, 0, unroll=True)

    yr = yret_s[...]
    lane = lax.broadcasted_iota(jnp.int32, (t, RET_WIDTH), 1) // RET_DK
    mu = jnp.zeros_like(yr)
    for hd in range(RET_HEADS):
        m = lane == hd
        s = jnp.sum(jnp.where(m, yr, 0.0), axis=-1, keepdims=True) * (1.0 / RET_DK)
        mu = jnp.where(m, s, mu)
    yc = yr - mu
    var = jnp.zeros_like(yr)
    for hd in range(RET_HEADS):
        m = lane == hd
        s = jnp.sum(jnp.where(m, yc * yc, 0.0), axis=-1, keepdims=True) * (1.0 / RET_DK)
        var = jnp.where(m, s, var)
    y_ret = yc * lax.rsqrt(var + EPS) * _silu(proj(OFF_G, RET_WIDTH))

    ys = (yssd_s[...] + dskip_xs) * _silu(proj(OFF_Z, SSD_WIDTH))
    parts = []
    for g in range(SSD_GROUPS):
        yg = ys[:, g * SSD_GROUP_WIDTH:(g + 1) * SSD_GROUP_WIDTH]
        parts.append(yg * lax.rsqrt(jnp.mean(yg * yg, axis=-1, keepdims=True) + EPS))
    y_ssd = jnp.concatenate(parts, axis=-1) * snw_ref[...]

    xc = _causal_conv(proj(OFF_LX, LRU_WIDTH), lx_buf, lcw_ref, lcb_ref, LRU_CONV)
    gates = _dot(xc.astype(BF16), lw_ref[...]) + lb_ref[...]
    r_gate = _sigmoid(gates[:, :LRU_WIDTH])
    i_gate = _sigmoid(gates[:, LRU_WIDTH:])
    log_a = (-LRU_C) * r_gate * _softplus(-lam_ref[...])
    a = jnp.exp(log_a)
    u = jnp.sqrt(-jnp.tanh(log_a) * (1.0 + a * a)) * (i_gate * xc)
    row = lax.broadcasted_iota(jnp.int32, (t, LRU_WIDTH), 0)
    u = u + jnp.where(row == 0, a * hlru_s[0:1, :], 0.0)
    step = 1
    while step < t:
        keep = row >= step
        a_sh = jnp.where(keep, pltpu.roll(a, step, axis=0), 1.0)
        u_sh = jnp.where(keep, pltpu.roll(u, step, axis=0), 0.0)
        u = a * u_sh + u
        a = a * a_sh
        step *= 2
    hlru_s[...] = u[t - SUBLANES:t, :][SUBLANES - 1:SUBLANES, :] * jnp.ones((SUBLANES, 1), F32)
    y_lru = u * _gelu_tanh(proj(OFF_LG, LRU_WIDTH))

    o = _dot(y_ret.astype(BF16), wout_ref[0:256, :])
    o = o + _dot(y_ssd.astype(BF16), wout_ref[256:768, :])
    o = o + _dot(y_lru.astype(BF16), wout_ref[768:1024, :])
    out_ref[...] = x + o


def _const_spec(shape):
    nd = len(shape)
    return pl.BlockSpec(shape, lambda b, j: (0,) * nd)


def _mixer_call(x, tables, p):
    bsz, seq, d = x.shape
    t = TOKENS_MIX
    cos_t, sin_t, dmask_t, qdec_t, kdec_t, cdec_t = tables
    xspec = pl.BlockSpec((None, t, d), lambda b, j: (b, j, 0))
    tspec = pl.BlockSpec((t, 128), lambda b, j: (j, 0))
    consts = [p['n1'], p['win'], dmask_t, qdec_t, kdec_t, cdec_t,
              p['scw'], p['scb'], p['dtb'], p['alog'], p['dsk'], p['snw'],
              p['lcw'], p['lcb'], p['lw'], p['lb'], p['lam'], p['wout']]
    scratch = [
        pltpu.VMEM((t, 256), F32), pltpu.VMEM((t, 256), F32), pltpu.VMEM((t, 256), F32),
        pltpu.VMEM((t, SSD_WIDTH), F32), pltpu.VMEM((t, SSD_WIDTH), F32),
        pltpu.VMEM((t, 256), F32), pltpu.VMEM((t, 256), F32),
        pltpu.VMEM((t, RET_WIDTH), F32), pltpu.VMEM((t, SSD_WIDTH), F32),
        pltpu.VMEM((256, 256), F32), pltpu.VMEM((SSD_GROUPS, SSD_STATE, SSD_GROUP_WIDTH), F32),
        pltpu.VMEM((SUBLANES, LRU_WIDTH), F32),
        pltpu.VMEM((SUBLANES, SSD_CONV_DIM), F32), pltpu.VMEM((SUBLANES, LRU_WIDTH), F32),
    ]
    return pl.pallas_call(
        _mixer_kernel,
        grid=(bsz, seq // t),
        in_specs=[xspec, tspec, tspec] + [_const_spec(c.shape) for c in consts],
        out_specs=xspec,
        out_shape=jax.ShapeDtypeStruct(x.shape, x.dtype),
        scratch_shapes=scratch,
        compiler_params=pltpu.CompilerParams(
            dimension_semantics=("parallel", "arbitrary"),
            vmem_limit_bytes=VMEM_LIMIT_BYTES),
        name="mixer",
    )(x, cos_t, sin_t, *consts)


def _ffn_kernel(x_ref, n2_ref, wup_ref, cw_ref, cb_ref, wdn_ref, fin_ref, out_ref, up_buf,
                *, final_norm):
    t = x_ref.shape[0]
    j = pl.program_id(1)

    @pl.when(j == 0)
    def _():
        up_buf[...] = jnp.zeros_like(up_buf)

    x = x_ref[...]
    h = _rmsnorm(x, n2_ref[...]).astype(BF16)
    n_chunks = D_FF // FFN_COLS

    def up_proj(cidx):
        return [_dot(h, wup_ref[:, base + cidx * FFN_COLS:base + (cidx + 1) * FFN_COLS])
                for base in (0, D_FF)]

    ups = [up_proj(c) for c in range(min(FFN_LOOKAHEAD, n_chunks))]
    acc = x
    for cidx in range(n_chunks):
        if cidx + FFN_LOOKAHEAD < n_chunks:
            ups.append(up_proj(cidx + FFN_LOOKAHEAD))
        halves = []
        for half, base in enumerate((0, D_FF)):
            cols = slice(base + cidx * FFN_COLS, base + (cidx + 1) * FFN_COLS)
            halves.append(_causal_conv(ups[cidx][half], up_buf, cw_ref, cb_ref, FFN_CONV, cols))
        ups[cidx] = None
        act = (_gelu_tanh(halves[0]) * halves[1]).astype(BF16)
        acc = acc + _dot(act, wdn_ref[cidx * FFN_COLS:(cidx + 1) * FFN_COLS, :])
    if final_norm:
        acc = _rmsnorm(acc, fin_ref[...])
    out_ref[...] = acc


def _ffn_call(x, p, final_w, final_norm):
    bsz, seq, d = x.shape
    t = TOKENS_FFN
    xspec = pl.BlockSpec((None, t, d), lambda b, j: (b, j, 0))
    consts = [p['n2'], p['wup'], p['fcw'], p['fcb'], p['wdn'], final_w]
    return pl.pallas_call(
        functools.partial(_ffn_kernel, final_norm=final_norm),
        grid=(bsz, seq // t),
        in_specs=[xspec] + [_const_spec(c.shape) for c in consts],
        out_specs=xspec,
        out_shape=jax.ShapeDtypeStruct(x.shape, x.dtype),
        scratch_shapes=[pltpu.VMEM((SUBLANES, 2 * D_FF), F32)],
        compiler_params=pltpu.CompilerParams(
            dimension_semantics=("parallel", "arbitrary"),
            vmem_limit_bytes=VMEM_LIMIT_BYTES),
        name="ffn",
    )(x, *consts)


def _retention_tables(seq):
    half = RET_DK // 2
    pos = jnp.arange(seq, dtype=F32)
    inv = ROPE_BASE ** (-jnp.arange(half, dtype=F32) / half)
    ang = pos[:, None] * inv[None, :]
    cos_t = jnp.tile(jnp.cos(ang), (1, RET_HEADS))
    sin_t = jnp.tile(jnp.sin(ang), (1, RET_HEADS))
    log_gamma = jnp.log(1.0 - 2.0 ** (-5.0 - jnp.arange(RET_HEADS, dtype=F32)))
    idx = jnp.arange(CHUNK, dtype=F32)
    diff = idx[:, None] - idx[None, :]
    dmask = jnp.where(diff >= 0, jnp.exp(log_gamma[:, None, None] * jnp.maximum(diff, 0.0)), 0.0)
    scale = RET_DK ** -0.5
    dmask_t = jnp.transpose(dmask, (1, 0, 2)).reshape(CHUNK, RET_HEADS * CHUNK) * scale
    qk_head = (np.arange(256) % 128) // 32
    q_decay = jnp.exp(log_gamma[:, None] * (idx + 1.0)[None, :])
    k_decay = jnp.exp(log_gamma[:, None] * (CHUNK - 1 - idx)[None, :])
    qdec_t = q_decay.T[:, qk_head] * scale
    kdec_t = k_decay.T[:, qk_head]
    cdec_t = jnp.exp(log_gamma * CHUNK)[np.arange(256) // RET_DK][None, :]
    return cos_t, sin_t, dmask_t, qdec_t, kdec_t, cdec_t


def _layer_params(l, norm1_w, w_in, ssd_conv_w, ssd_conv_b, ssd_dt_bias, ssd_a_log, ssd_d,
                  ssd_norm_w, lru_conv_w, lru_conv_b, lru_w_a, lru_b_a, lru_w_x, lru_b_x,
                  lru_lambda, w_out, norm2_w, ffn_w_up, ffn_conv_w, ffn_conv_b, ffn_w_down):
    w = w_in[l]
    j = np.arange(256)
    qk_perm = ((j % 128) // 32) * RET_DK + (j // 128) * (RET_DK // 2) + j % 32
    head_of_lane = np.arange(SSD_WIDTH) // (SSD_WIDTH // SSD_HEADS)
    o = 0
    wq = w[:, o:o + 256][:, qk_perm]; o += 256
    wk = w[:, o:o + 256][:, qk_perm]; o += 256
    wv = w[:, o:o + 256]; o += 256
    wg = w[:, o:o + 256]; o += 256
    wz = w[:, o:o + SSD_WIDTH]; o += SSD_WIDTH
    wxbc = w[:, o:o + SSD_CONV_DIM]; o += SSD_CONV_DIM
    wdt = w[:, o:o + SSD_HEADS][:, head_of_lane]; o += SSD_HEADS
    wlg = w[:, o:o + LRU_WIDTH]; o += LRU_WIDTH
    wlx = w[:, o:o + LRU_WIDTH]
    win = jnp.concatenate([wq, wk, wv, wg, wz, wxbc, wdt, wlg, wlx], axis=1).astype(BF16)
    blk = LRU_WIDTH // LRU_BLOCKS

    def block_diag(wb):
        full = jnp.zeros((LRU_WIDTH, LRU_WIDTH), F32)
        for kb in range(LRU_BLOCKS):
            full = full.at[kb * blk:(kb + 1) * blk, kb * blk:(kb + 1) * blk].set(wb[kb])
        return full

    lw = jnp.concatenate([block_diag(lru_w_a[l]), block_diag(lru_w_x[l])], axis=1).astype(BF16)
    lb = jnp.concatenate([lru_b_a[l], lru_b_x[l]])[None, :]
    return dict(
        n1=norm1_w[l][None, :], win=win,
        scw=ssd_conv_w[l], scb=ssd_conv_b[l][None, :],
        dtb=ssd_dt_bias[l][head_of_lane][None, :], alog=ssd_a_log[l][head_of_lane][None, :],
        dsk=ssd_d[l][head_of_lane][None, :], snw=ssd_norm_w[l][None, :],
        lcw=lru_conv_w[l], lcb=lru_conv_b[l][None, :], lw=lw, lb=lb,
        lam=lru_lambda[l][None, :], wout=w_out[l].astype(BF16),
        n2=norm2_w[l][None, :], wup=ffn_w_up[l].astype(BF16), fcw=ffn_conv_w[l],
        fcb=ffn_conv_b[l][None, :], wdn=ffn_w_down[l].astype(BF16))


def kernel(x, norm1_w, w_in, ssd_conv_w, ssd_conv_b, ssd_dt_bias, ssd_a_log, ssd_d, ssd_norm_w, lru_conv_w, lru_conv_b, lru_w_a, lru_b_a, lru_w_x, lru_b_x, lru_lambda, w_out, norm2_w, ffn_w_up, ffn_conv_w, ffn_conv_b, ffn_w_down, final_norm_w):
    depth = w_in.shape[0]
    seq = x.shape[1]
    assert seq % TOKENS_MIX == 0 and seq % TOKENS_FFN == 0 and x.shape[2] == D_MODEL
    tables = _retention_tables(seq)
    fin = final_norm_w[None, :]
    for l in range(depth):
        p = _layer_params(l, norm1_w, w_in, ssd_conv_w, ssd_conv_b, ssd_dt_bias, ssd_a_log, ssd_d,
                          ssd_norm_w, lru_conv_w, lru_conv_b, lru_w_a, lru_b_a, lru_w_x, lru_b_x,
                          lru_lambda, w_out, norm2_w, ffn_w_up, ffn_conv_w, ffn_conv_b, ffn_w_down)
        x = _mixer_call(x, tables, p)
        x = _ffn_call(x, p, fin, final_norm=(l == depth - 1))
    return x
```

```python
import functools
import math

import jax
import jax.numpy as jnp
import numpy as np
from jax import lax
from jax.experimental import pallas as pl
from jax.experimental.pallas import tpu as pltpu

D_MODEL = 1024
CHUNK = 64
EPS = 1e-6

RET_HEADS = 4
RET_DK = 64
RET_WIDTH = 256
ROPE_BASE = 10000.0

SSD_HEADS = 8
SSD_WIDTH = 512
SSD_GROUPS = 2
SSD_STATE = 128
SSD_CONV = 4
SSD_CONV_DIM = 1024
SSD_GROUP_WIDTH = SSD_WIDTH // SSD_GROUPS

LRU_WIDTH = 256
LRU_BLOCKS = 4
LRU_CONV = 4
LRU_C = 8.0

D_FF = 2816
FFN_CONV = 3

SUBLANES = 8
VMEM_LIMIT_BYTES = 56 * 1024 * 1024

OFF_Q = 0
OFF_K = 256
OFF_V = 512
OFF_G = 768
OFF_Z = 1024
OFF_XBC = 1536
OFF_DT = 2560
OFF_LG = 3072
OFF_LX = 3328
D_PROJ_PAD = 3584

TOKENS_MIX = 512
TOKENS_FFN = 512
FFN_COLS = 768
FFN_LOOKAHEAD = 2
CUMSUM_ROWS = 256

F32 = jnp.float32
BF16 = jnp.bfloat16


def _dot(a, b):
    return jnp.dot(a, b, preferred_element_type=F32)


def _dot_nt(a, b):
    return lax.dot_general(a, b, (((1,), (1,)), ((), ())), preferred_element_type=F32)


def _dot_tn(a, b):
    return lax.dot_general(a, b, (((0,), (0,)), ((), ())), preferred_element_type=F32)


def _sigmoid(x):
    return 1.0 / (1.0 + jnp.exp(-x))


def _silu(x):
    return x * _sigmoid(x)


def _softplus(x):
    return jnp.maximum(x, 0.0) + jnp.log(1.0 + jnp.exp(-jnp.abs(x)))


def _gelu_tanh(x):
    c = math.sqrt(2.0 / math.pi)
    return 0.5 * x * (1.0 + jnp.tanh(c * (x + 0.044715 * (x * x * x))))


def _rmsnorm(x, w):
    ms = jnp.mean(x * x, axis=-1, keepdims=True)
    return x * lax.rsqrt(ms + EPS) * w


def _split3_bf16(x):
    hi = x.astype(BF16)
    r1 = x - hi.astype(F32)
    mid = r1.astype(BF16)
    lo = (r1 - mid.astype(F32)).astype(BF16)
    return hi, mid, lo


def _causal_conv(cur, tail_ref, w_ref, b_ref, width, cols=None):
    cs = slice(None) if cols is None else cols
    t = cur.shape[0]
    row = lax.broadcasted_iota(jnp.int32, cur.shape, 0)
    prev = tail_ref[:, cs]
    acc = cur * w_ref[width - 1:width, cs] + b_ref[0:1, cs]
    for back in range(1, width):
        shifted = pltpu.roll(cur, back, axis=0)
        head = pltpu.roll(prev, back, axis=0)
        head = jnp.concatenate([head, jnp.zeros((t - SUBLANES, cur.shape[1]), F32)], axis=0)
        shifted = jnp.where(row < back, head, shifted)
        acc = acc + shifted * w_ref[width - 1 - back:width - back, cs]
    tail_ref[:, cs] = cur[t - SUBLANES:, :]
    return acc


def _mixer_kernel(x_ref, cos_ref, sin_ref, n1_ref, win_ref,
                  dmask_ref, qdec_ref, kdec_ref, cdec_ref,
                  tril_ref, mqk_ref, mbd_ref, mst_ref, gavg_ref,
                  scw_ref, scb_ref, dtb_ref, alog_ref, dsk_ref, snw_ref,
                  lcw_ref, lcb_ref, lw_ref, lb_ref, lam_ref, wout_ref,
                  out_ref,
                  sret_s, sssd_s, hlru_s, xbc_buf, lx_buf):
    t = x_ref.shape[0]
    n_chunks = t // CHUNK
    j = pl.program_id(1)

    @pl.when(j == 0)
    def _():
        sret_s[...] = jnp.zeros_like(sret_s)
        sssd_s[...] = jnp.zeros_like(sssd_s)
        hlru_s[...] = jnp.zeros_like(hlru_s)
        xbc_buf[...] = jnp.zeros_like(xbc_buf)
        lx_buf[...] = jnp.zeros_like(lx_buf)

    x = x_ref[...]
    h = _rmsnorm(x, n1_ref[...]).astype(BF16)

    def proj(off, width):
        return _dot(h, win_ref[:, off:off + width])

    dt = _softplus(proj(OFF_DT, SSD_WIDTH) + dtb_ref[...])
    da = dt * (-jnp.exp(alog_ref[...]))
    d_hi, d_mid, d_lo = _split3_bf16(da)
    xbc = _silu(_causal_conv(proj(OFF_XBC, SSD_CONV_DIM), xbc_buf, scw_ref, scb_ref, SSD_CONV))
    xs = xbc[:, :SSD_WIDTH]
    bm = xbc[:, SSD_WIDTH:SSD_WIDTH + SSD_GROUPS * SSD_STATE]
    cm = xbc[:, SSD_WIDTH + SSD_GROUPS * SSD_STATE:].astype(BF16)
    tril = tril_ref[...]
    tb = tril.shape[0]
    cum = jnp.concatenate(
        [sum(_dot(tril, part[r0:r0 + tb]) for part in (d_hi, d_mid, d_lo)) for r0 in range(0, t, tb)],
        axis=0)
    xdt = xs * dt

    xc = _causal_conv(proj(OFF_LX, LRU_WIDTH), lx_buf, lcw_ref, lcb_ref, LRU_CONV)
    gates = _dot(xc.astype(BF16), lw_ref[...]) + lb_ref[...]
    r_gate = _sigmoid(gates[:, :LRU_WIDTH])
    i_gate = _sigmoid(gates[:, LRU_WIDTH:])
    log_a = (-LRU_C) * r_gate * _softplus(-lam_ref[...])
    a = jnp.exp(log_a)
    u = jnp.sqrt(-jnp.tanh(log_a) * (1.0 + a * a)) * (i_gate * xc)
    groups = t // SUBLANES
    a3 = a.reshape(groups, SUBLANES, LRU_WIDTH)
    u3 = u.reshape(groups, SUBLANES, LRU_WIDTH)
    sub = lax.broadcasted_iota(jnp.int32, (groups, SUBLANES, LRU_WIDTH), 1)
    step = 1
    while step < SUBLANES:
        keep = sub >= step
        a_sh = jnp.where(keep, pltpu.roll(a3, step, axis=1), 1.0)
        u_sh = jnp.where(keep, pltpu.roll(u3, step, axis=1), 0.0)
        u3 = a3 * u_sh + u3
        a3 = a3 * a_sh
        step *= 2
    carry = hlru_s[...]
    outs = []
    for g in range(groups):
        hg = a3[g] * carry + u3[g]
        outs.append(hg)
        carry = jnp.broadcast_to(hg[SUBLANES - 1:SUBLANES, :], (SUBLANES, LRU_WIDTH))
    hlru_s[...] = carry
    h_lru = jnp.concatenate(outs, axis=0)

    cos = cos_ref[...]
    sin = sin_ref[...]

    def rope(p):
        p1, p2 = p[:, :128], p[:, 128:]
        return jnp.concatenate([p1 * cos - p2 * sin, p1 * sin + p2 * cos], axis=-1)

    q = rope(proj(OFF_Q, 256))
    k = rope(proj(OFF_K, 256))
    v = proj(OFF_V, 256)

    mask_qk = mqk_ref[...]
    mask_bd = mbd_ref[...]
    rc = lax.broadcasted_iota(jnp.int32, (CHUNK, 256), 0)
    lc = lax.broadcasted_iota(jnp.int32, (CHUNK, 256), 1) % CHUNK
    mask_tri = rc >= lc
    mask_diag = rc == lc

    def tile4(a):
        return jnp.concatenate([a, a, a, a], axis=0)

    ret1 = []
    ssd1 = []
    for c in range(n_chunks):
        rows = slice(c * CHUNK, (c + 1) * CHUNK)
        qc, kc, vc = q[rows], k[rows], v[rows]
        kt4 = tile4(kc.astype(BF16)) * mask_qk
        p_c = (_dot_nt(qc.astype(BF16), kt4) * dmask_ref[...]).astype(BF16)
        kv = _dot_tn((kc * kdec_ref[...]).astype(BF16), vc.astype(BF16)) * mst_ref[...]
        vbd = tile4(vc.astype(BF16)) * mask_bd
        ret1.append((p_c, kv, vbd, (qc * qdec_ref[...]).astype(BF16)))
        per_group = []
        for g in range(SSD_GROUPS):
            lanes = slice(g * SSD_GROUP_WIDTH, (g + 1) * SSD_GROUP_WIDTH)
            nl = slice(g * SSD_STATE, (g + 1) * SSD_STATE)
            cumc = cum[rows, lanes]
            xdtc = xdt[rows, lanes]
            bc = bm[rows, nl].astype(BF16)
            cc = cm[rows, nl]
            cum_row = jnp.sum(jnp.where(mask_diag, cumc, 0.0), axis=0, keepdims=True)
            lmat = jnp.where(mask_tri, jnp.exp(cumc - cum_row), 0.0)
            m_cg = (_dot_nt(cc, tile4(bc)) * lmat).astype(BF16)
            cum_last = cumc[CHUNK - 1:CHUNK, :]
            upd = _dot_tn(bc, (xdtc * jnp.exp(cum_last - cumc)).astype(BF16))
            xbd = tile4(xdtc.astype(BF16)) * mask_bd
            per_group.append((m_cg, upd, xbd, cc, jnp.exp(cumc), jnp.exp(cum_last)))
        ssd1.append(per_group)

    g_proj = proj(OFF_G, RET_WIDTH)
    z_proj = proj(OFF_Z, SSD_WIDTH)
    lg_proj = proj(OFF_LG, LRU_WIDTH)

    s_ret = [sret_s[...]]
    s_ssd = [[sssd_s[g] for g in range(SSD_GROUPS)]]
    for c in range(n_chunks):
        s_ret.append(s_ret[c] * cdec_ref[...] + ret1[c][1])
        s_ssd.append([s_ssd[c][g] * ssd1[c][g][5] + ssd1[c][g][1] for g in range(SSD_GROUPS)])
    sret_s[...] = s_ret[n_chunks]
    for g in range(SSD_GROUPS):
        sssd_s[g] = s_ssd[n_chunks][g]

    yr_parts = []
    ys_parts = []
    for c in range(n_chunks):
        p_c, _, vbd, qd = ret1[c]
        yr_parts.append(_dot(p_c, vbd) + _dot(qd, s_ret[c].astype(BF16)))
        row_parts = []
        for g in range(SSD_GROUPS):
            m_cg, _, xbd, cc, ecum, _ = ssd1[c][g]
            row_parts.append(_dot(m_cg, xbd) + ecum * _dot(cc, s_ssd[c][g].astype(BF16)))
        ys_parts.append(jnp.concatenate(row_parts, axis=1))
    yr = jnp.concatenate(yr_parts, axis=0)
    yssd = jnp.concatenate(ys_parts, axis=0)

    gavg = gavg_ref[...]

    def head_mean(a):
        hi = a.astype(BF16)
        lo = (a - hi.astype(F32)).astype(BF16)
        return _dot(hi, gavg) + _dot(lo, gavg)

    yc = yr - head_mean(yr)
    var = head_mean(yc * yc)
    y_ret = yc * lax.rsqrt(var + EPS) * _silu(g_proj)

    ys = (yssd + xs * dsk_ref[...]) * _silu(z_proj)
    parts = []
    for g in range(SSD_GROUPS):
        yg = ys[:, g * SSD_GROUP_WIDTH:(g + 1) * SSD_GROUP_WIDTH]
        parts.append(yg * lax.rsqrt(jnp.mean(yg * yg, axis=-1, keepdims=True) + EPS))
    y_ssd = jnp.concatenate(parts, axis=-1) * snw_ref[...]

    y_lru = h_lru * _gelu_tanh(lg_proj)

    o = _dot(y_ret.astype(BF16), wout_ref[0:256, :])
    o = o + _dot(y_ssd.astype(BF16), wout_ref[256:768, :])
    o = o + _dot(y_lru.astype(BF16), wout_ref[768:1024, :])
    out_ref[...] = x + o


def _const_spec(shape):
    nd = len(shape)
    return pl.BlockSpec(shape, lambda b, j: (0,) * nd, pipeline_mode=pl.Buffered(1))


def _mixer_call(x, tables, p):
    bsz, seq, d = x.shape
    t = TOKENS_MIX
    cos_t, sin_t, dmask_t, qdec_t, kdec_t, cdec_t = tables
    masks = _mixer_masks(min(t, CUMSUM_ROWS))
    xspec = pl.BlockSpec((None, t, d), lambda b, j: (b, j, 0))
    tspec = pl.BlockSpec((t, 128), lambda b, j: (j, 0))
    consts = [p['n1'], p['win'], dmask_t, qdec_t, kdec_t, cdec_t, *masks,
              p['scw'], p['scb'], p['dtb'], p['alog'], p['dsk'], p['snw'],
              p['lcw'], p['lcb'], p['lw'], p['lb'], p['lam'], p['wout']]
    scratch = [
        pltpu.VMEM((256, 256), F32), pltpu.VMEM((SSD_GROUPS, SSD_STATE, SSD_GROUP_WIDTH), F32),
        pltpu.VMEM((SUBLANES, LRU_WIDTH), F32),
        pltpu.VMEM((SUBLANES, SSD_CONV_DIM), F32), pltpu.VMEM((SUBLANES, LRU_WIDTH), F32),
    ]
    return pl.pallas_call(
        _mixer_kernel,
        grid=(bsz, seq // t),
        in_specs=[xspec, tspec, tspec] + [_const_spec(c.shape) for c in consts],
        out_specs=xspec,
        out_shape=jax.ShapeDtypeStruct(x.shape, x.dtype),
        scratch_shapes=scratch,
        compiler_params=pltpu.CompilerParams(
            dimension_semantics=("parallel", "arbitrary"),
            vmem_limit_bytes=VMEM_LIMIT_BYTES),
        name="mixer",
    )(x, cos_t, sin_t, *consts)


def _ffn_kernel(x_ref, n2_ref, wup_ref, cw_ref, cb_ref, wdn_ref, fin_ref, out_ref, up_buf,
                *, final_norm):
    j = pl.program_id(1)

    @pl.when(j == 0)
    def _():
        up_buf[...] = jnp.zeros_like(up_buf)

    x = x_ref[...]
    h = _rmsnorm(x, n2_ref[...]).astype(BF16)
    bounds = list(range(0, D_FF, FFN_COLS)) + [D_FF]
    n_chunks = len(bounds) - 1

    def up_proj(cidx):
        return [_dot(h, wup_ref[:, base + bounds[cidx]:base + bounds[cidx + 1]])
                for base in (0, D_FF)]

    ups = [up_proj(c) for c in range(min(FFN_LOOKAHEAD, n_chunks))]
    acc = x
    for cidx in range(n_chunks):
        if cidx + FFN_LOOKAHEAD < n_chunks:
            ups.append(up_proj(cidx + FFN_LOOKAHEAD))
        halves = []
        for half, base in enumerate((0, D_FF)):
            cols = slice(base + bounds[cidx], base + bounds[cidx + 1])
            halves.append(_causal_conv(ups[cidx][half], up_buf, cw_ref, cb_ref, FFN_CONV, cols))
        ups[cidx] = None
        act = (_gelu_tanh(halves[0]) * halves[1]).astype(BF16)
        acc = acc + _dot(act, wdn_ref[bounds[cidx]:bounds[cidx + 1], :])
    if final_norm:
        acc = _rmsnorm(acc, fin_ref[...])
    out_ref[...] = acc


def _ffn_call(x, p, final_w, final_norm):
    bsz, seq, d = x.shape
    t = TOKENS_FFN
    xspec = pl.BlockSpec((None, t, d), lambda b, j: (b, j, 0))
    consts = [p['n2'], p['wup'], p['fcw'], p['fcb'], p['wdn'], final_w]
    return pl.pallas_call(
        functools.partial(_ffn_kernel, final_norm=final_norm),
        grid=(bsz, seq // t),
        in_specs=[xspec] + [_const_spec(c.shape) for c in consts],
        out_specs=xspec,
        out_shape=jax.ShapeDtypeStruct(x.shape, x.dtype),
        scratch_shapes=[pltpu.VMEM((SUBLANES, 2 * D_FF), F32)],
        compiler_params=pltpu.CompilerParams(
            dimension_semantics=("parallel", "arbitrary"),
            vmem_limit_bytes=VMEM_LIMIT_BYTES),
        name="ffn",
    )(x, *consts)


def _mixer_masks(t):
    r = np.arange(t)
    tril = ((r[:, None] >= r[None, :]) & (r[:, None] // CHUNK == r[None, :] // CHUNK))
    rb = np.arange(4 * CHUNK)[:, None] // CHUNK
    ln = np.arange(256)[None, :]
    mask_qk = rb == (ln % 128) // 32
    mask_bd = rb == ln // CHUNK
    mask_state = ((np.arange(256)[:, None] % 128) // 32) == ln // CHUNK
    gavg = (ln // RET_DK == np.arange(256)[:, None] // RET_DK) * (1.0 / RET_DK)
    return (jnp.asarray(tril, BF16), jnp.asarray(mask_qk, BF16), jnp.asarray(mask_bd, BF16),
            jnp.asarray(mask_state, F32), jnp.asarray(gavg, BF16))


def _retention_tables(seq):
    half = RET_DK // 2
    pos = jnp.arange(seq, dtype=F32)
    inv = ROPE_BASE ** (-jnp.arange(half, dtype=F32) / half)
    ang = pos[:, None] * inv[None, :]
    cos_t = jnp.tile(jnp.cos(ang), (1, RET_HEADS))
    sin_t = jnp.tile(jnp.sin(ang), (1, RET_HEADS))
    log_gamma = jnp.log(1.0 - 2.0 ** (-5.0 - jnp.arange(RET_HEADS, dtype=F32)))
    idx = jnp.arange(CHUNK, dtype=F32)
    diff = idx[:, None] - idx[None, :]
    dmask = jnp.where(diff >= 0, jnp.exp(log_gamma[:, None, None] * jnp.maximum(diff, 0.0)), 0.0)
    scale = RET_DK ** -0.5
    dmask_t = jnp.transpose(dmask, (1, 0, 2)).reshape(CHUNK, RET_HEADS * CHUNK) * scale
    qk_head = (np.arange(256) % 128) // 32
    q_decay = jnp.exp(log_gamma[:, None] * (idx + 1.0)[None, :])
    k_decay = jnp.exp(log_gamma[:, None] * (CHUNK - 1 - idx)[None, :])
    qdec_t = q_decay.T[:, qk_head] * scale
    kdec_t = k_decay.T[:, qk_head]
    cdec_t = jnp.exp(log_gamma * CHUNK)[np.arange(256) // RET_DK][None, :]
    return cos_t, sin_t, dmask_t, qdec_t, kdec_t, cdec_t


def _layer_params(l, norm1_w, w_in, ssd_conv_w, ssd_conv_b, ssd_dt_bias, ssd_a_log, ssd_d,
                  ssd_norm_w, lru_conv_w, lru_conv_b, lru_w_a, lru_b_a, lru_w_x, lru_b_x,
                  lru_lambda, w_out, norm2_w, ffn_w_up, ffn_conv_w, ffn_conv_b, ffn_w_down):
    w = w_in[l]
    j = np.arange(256)
    qk_perm = ((j % 128) // 32) * RET_DK + (j // 128) * (RET_DK // 2) + j % 32
    head_of_lane = np.arange(SSD_WIDTH) // (SSD_WIDTH // SSD_HEADS)
    o = 0
    wq = w[:, o:o + 256][:, qk_perm]; o += 256
    wk = w[:, o:o + 256][:, qk_perm]; o += 256
    wv = w[:, o:o + 256]; o += 256
    wg = w[:, o:o + 256]; o += 256
    wz = w[:, o:o + SSD_WIDTH]; o += SSD_WIDTH
    wxbc = w[:, o:o + SSD_CONV_DIM]; o += SSD_CONV_DIM
    wdt = w[:, o:o + SSD_HEADS][:, head_of_lane]; o += SSD_HEADS
    wlg = w[:, o:o + LRU_WIDTH]; o += LRU_WIDTH
    wlx = w[:, o:o + LRU_WIDTH]
    win = jnp.concatenate([wq, wk, wv, wg, wz, wxbc, wdt, wlg, wlx], axis=1).astype(BF16)
    blk = LRU_WIDTH // LRU_BLOCKS

    def block_diag(wb):
        full = jnp.zeros((LRU_WIDTH, LRU_WIDTH), F32)
        for kb in range(LRU_BLOCKS):
            full = full.at[kb * blk:(kb + 1) * blk, kb * blk:(kb + 1) * blk].set(wb[kb])
        return full

    lw = jnp.concatenate([block_diag(lru_w_a[l]), block_diag(lru_w_x[l])], axis=1).astype(BF16)
    lb = jnp.concatenate([lru_b_a[l], lru_b_x[l]])[None, :]
    return dict(
        n1=norm1_w[l][None, :], win=win,
        scw=ssd_conv_w[l], scb=ssd_conv_b[l][None, :],
        dtb=ssd_dt_bias[l][head_of_lane][None, :], alog=ssd_a_log[l][head_of_lane][None, :],
        dsk=ssd_d[l][head_of_lane][None, :], snw=ssd_norm_w[l][None, :],
        lcw=lru_conv_w[l], lcb=lru_conv_b[l][None, :], lw=lw, lb=lb,
        lam=lru_lambda[l][None, :], wout=w_out[l].astype(BF16),
        n2=norm2_w[l][None, :], wup=ffn_w_up[l].astype(BF16), fcw=ffn_conv_w[l],
        fcb=ffn_conv_b[l][None, :], wdn=ffn_w_down[l].astype(BF16))


def kernel(x, norm1_w, w_in, ssd_conv_w, ssd_conv_b, ssd_dt_bias, ssd_a_log, ssd_d, ssd_norm_w, lru_conv_w, lru_conv_b, lru_w_a, lru_b_a, lru_w_x, lru_b_x, lru_lambda, w_out, norm2_w, ffn_w_up, ffn_conv_w, ffn_conv_b, ffn_w_down, final_norm_w):
    depth = w_in.shape[0]
    seq = x.shape[1]
    assert seq % TOKENS_MIX == 0 and seq % TOKENS_FFN == 0 and x.shape[2] == D_MODEL
    tables = _retention_tables(seq)
    fin = final_norm_w[None, :]
    for l in range(depth):
        p = _layer_params(l, norm1_w, w_in, ssd_conv_w, ssd_conv_b, ssd_dt_bias, ssd_a_log, ssd_d,
                          ssd_norm_w, lru_conv_w, lru_conv_b, lru_w_a, lru_b_a, lru_w_x, lru_b_x,
                          lru_lambda, w_out, norm2_w, ffn_w_up, ffn_conv_w, ffn_conv_b, ffn_w_down)
        x = _mixer_call(x, tables, p)
        x = _ffn_call(x, p, fin, final_norm=(l == depth - 1))
    return x
```

```python
import functools
import math

import jax
import jax.numpy as jnp
import numpy as np
from jax import lax
from jax.experimental import pallas as pl
from jax.experimental.pallas import tpu as pltpu

D_MODEL = 1024
CHUNK = 64
EPS = 1e-6

RET_HEADS = 4
RET_DK = 64
RET_WIDTH = 256
ROPE_BASE = 10000.0

SSD_HEADS = 8
SSD_WIDTH = 512
SSD_GROUPS = 2
SSD_STATE = 128
SSD_CONV = 4
SSD_CONV_DIM = 1024
SSD_GROUP_WIDTH = SSD_WIDTH // SSD_GROUPS

LRU_WIDTH = 256
LRU_BLOCKS = 4
LRU_CONV = 4
LRU_C = 8.0

D_FF = 2816
FFN_CONV = 3

SUBLANES = 8
VMEM_LIMIT_BYTES = 56 * 1024 * 1024

OFF_Q = 0
OFF_K = 256
OFF_V = 512
OFF_G = 768
OFF_Z = 1024
OFF_XBC = 1536
OFF_DT = 2560
OFF_LG = 3072
OFF_LX = 3328
D_PROJ_PAD = 3584

TOKENS_MIX = 512
TOKENS_FFN = 512
FFN_COLS = 768
FFN_LOOKAHEAD = 2
CUMSUM_ROWS = 256

F32 = jnp.float32
BF16 = jnp.bfloat16


def _dot(a, b):
    return jnp.dot(a, b, preferred_element_type=F32)


def _dot_nt(a, b):
    return lax.dot_general(a, b, (((1,), (1,)), ((), ())), preferred_element_type=F32)


def _dot_tn(a, b):
    return lax.dot_general(a, b, (((0,), (0,)), ((), ())), preferred_element_type=F32)


def _sigmoid(x):
    return 1.0 / (1.0 + jnp.exp(-x))


def _silu(x):
    return x * _sigmoid(x)


def _softplus(x):
    return jnp.maximum(x, 0.0) + jnp.log(1.0 + jnp.exp(-jnp.abs(x)))


def _gelu_tanh(x):
    c = math.sqrt(2.0 / math.pi)
    return 0.5 * x * (1.0 + jnp.tanh(c * (x + 0.044715 * (x * x * x))))


def _gelu_tanh_x2(x):
    c = math.sqrt(2.0 / math.pi)
    return x * (1.0 + jnp.tanh(x * (c + (c * 0.044715) * (x * x))))


def _rmsnorm(x, w):
    ms = jnp.mean(x * x, axis=-1, keepdims=True)
    return x * lax.rsqrt(ms + EPS) * w


def _split3_bf16(x):
    hi = x.astype(BF16)
    r1 = x - hi.astype(F32)
    mid = r1.astype(BF16)
    lo = (r1 - mid.astype(F32)).astype(BF16)
    return hi, mid, lo


def _causal_conv(cur, tail_ref, w_ref, b_ref, width, cols=None):
    cs = slice(None) if cols is None else cols
    t = cur.shape[0]
    sub = lax.broadcasted_iota(jnp.int32, (SUBLANES, cur.shape[1]), 0)
    prev = tail_ref[:, cs]
    acc = cur * w_ref[width - 1:width, cs] + b_ref[0:1, cs]
    for back in range(1, width):
        shifted = pltpu.roll(cur, back, axis=0)
        head = jnp.where(sub < back, pltpu.roll(prev, back, axis=0), shifted[:SUBLANES])
        shifted = jnp.concatenate([head, shifted[SUBLANES:]], axis=0)
        acc = acc + shifted * w_ref[width - 1 - back:width - back, cs]
    tail_ref[:, cs] = cur[t - SUBLANES:, :]
    return acc


PERM_STRIDE = 4
PERM_BLOCK = SUBLANES * PERM_STRIDE


def _perm_row_start(vreg_row):
    return PERM_BLOCK * (vreg_row // PERM_STRIDE) + vreg_row % PERM_STRIDE


def _causal_conv_perm(cur, tail_ref, w_ref, b_ref, width, cols):
    t = cur.shape[0]
    n_rows = t // SUBLANES
    sub = lax.broadcasted_iota(jnp.int32, (SUBLANES, cur.shape[1]), 0)
    vrows = [cur[SUBLANES * kk:SUBLANES * (kk + 1)] for kk in range(n_rows)]
    prev = [tail_ref[SUBLANES * kk:SUBLANES * (kk + 1), cols] for kk in range(PERM_STRIDE)]
    rolled = {}

    def down_one(blk, kk):
        if (blk, kk) not in rolled:
            src = prev[kk] if blk < 0 else vrows[blk * PERM_STRIDE + kk]
            rolled[(blk, kk)] = pltpu.roll(src, 1, axis=0)
        return rolled[(blk, kk)]

    fixes = {}

    def fix(blk, kk):
        if (blk, kk) not in fixes:
            fixes[(blk, kk)] = jnp.where(sub == 0, down_one(blk - 1, kk), down_one(blk, kk))
        return fixes[(blk, kk)]

    acc = cur * w_ref[width - 1:width, cols] + b_ref[0:1, cols]
    for back in range(1, width):
        pieces = []
        for kk in range(n_rows):
            blk, off = divmod(kk, PERM_STRIDE)
            pieces.append(vrows[kk - back] if off >= back else fix(blk, off - back + PERM_STRIDE))
        acc = acc + jnp.concatenate(pieces, axis=0) * w_ref[width - 1 - back:width - back, cols]
    tail_ref[:, cols] = cur[t - PERM_BLOCK:, :]
    return acc


def _mixer_kernel(x_ref, cos_ref, sin_ref, n1_ref, win_ref,
                  dmask_ref, qdec_ref, kdec_ref, cdec_ref,
                  tril_ref, mqk_ref, mbd_ref, mst_ref, gavg_ref,
                  scw_ref, scb_ref, dtb_ref, alog_ref, dsk_ref, snw_ref,
                  lcw_ref, lcb_ref, lw_ref, lb_ref, lam_ref, wout_ref,
                  out_ref,
                  sret_s, sssd_s, hlru_s, xbc_buf, lx_buf):
    t = x_ref.shape[0]
    n_chunks = t // CHUNK
    j = pl.program_id(1)

    @pl.when(j == 0)
    def _():
        sret_s[...] = jnp.zeros_like(sret_s)
        sssd_s[...] = jnp.zeros_like(sssd_s)
        hlru_s[...] = jnp.zeros_like(hlru_s)
        xbc_buf[...] = jnp.zeros_like(xbc_buf)
        lx_buf[...] = jnp.zeros_like(lx_buf)

    x = x_ref[...]
    h = _rmsnorm(x, n1_ref[...]).astype(BF16)

    def proj(off, width):
        return _dot(h, win_ref[:, off:off + width])

    lx_proj = proj(OFF_LX, LRU_WIDTH)
    dt_proj = proj(OFF_DT, SSD_WIDTH)
    xbc_proj = proj(OFF_XBC, SSD_CONV_DIM)

    xc = _causal_conv(lx_proj, lx_buf, lcw_ref, lcb_ref, LRU_CONV)
    gates = _dot(xc.astype(BF16), lw_ref[...]) + lb_ref[...]
    r_gate = _sigmoid(gates[:, :LRU_WIDTH])
    i_gate = _sigmoid(gates[:, LRU_WIDTH:])
    log_a = (-LRU_C) * r_gate * _softplus(-lam_ref[...])
    a = jnp.exp(log_a)
    u = jnp.sqrt(-jnp.tanh(log_a) * (1.0 + a * a)) * (i_gate * xc)
    groups = t // SUBLANES
    a3 = a.reshape(groups, SUBLANES, LRU_WIDTH)
    u3 = u.reshape(groups, SUBLANES, LRU_WIDTH)
    sub = lax.broadcasted_iota(jnp.int32, (groups, SUBLANES, LRU_WIDTH), 1)
    step = 1
    while step < SUBLANES:
        keep = sub >= step
        a_sh = jnp.where(keep, pltpu.roll(a3, step, axis=1), 1.0)
        u_sh = jnp.where(keep, pltpu.roll(u3, step, axis=1), 0.0)
        u3 = a3 * u_sh + u3
        a3 = a3 * a_sh
        step *= 2
    carry = hlru_s[...]
    outs = []
    for g in range(groups):
        hg = a3[g] * carry + u3[g]
        outs.append(hg)
        carry = jnp.broadcast_to(hg[SUBLANES - 1:SUBLANES, :], (SUBLANES, LRU_WIDTH))
    hlru_s[...] = carry
    h_lru = jnp.concatenate(outs, axis=0)

    dt = _softplus(dt_proj + dtb_ref[...])
    da = dt * (-jnp.exp(alog_ref[...]))
    d_hi, d_mid, d_lo = _split3_bf16(da)
    xbc = _silu(_causal_conv(xbc_proj, xbc_buf, scw_ref, scb_ref, SSD_CONV))
    xs = xbc[:, :SSD_WIDTH]
    bm = xbc[:, SSD_WIDTH:SSD_WIDTH + SSD_GROUPS * SSD_STATE]
    cm = xbc[:, SSD_WIDTH + SSD_GROUPS * SSD_STATE:].astype(BF16)
    tril = tril_ref[...]
    tb = tril.shape[0]
    cum = jnp.concatenate(
        [sum(_dot(tril, part[r0:r0 + tb]) for part in (d_hi, d_mid, d_lo)) for r0 in range(0, t, tb)],
        axis=0)
    xdt = xs * dt

    cos = cos_ref[...]
    sin = sin_ref[...]

    def rope(p):
        p1, p2 = p[:, :128], p[:, 128:]
        return jnp.concatenate([p1 * cos - p2 * sin, p1 * sin + p2 * cos], axis=-1)

    q = rope(proj(OFF_Q, 256))
    k = rope(proj(OFF_K, 256))
    v = proj(OFF_V, 256)

    mask_qk = mqk_ref[...]
    mask_bd = mbd_ref[...]
    rc = lax.broadcasted_iota(jnp.int32, (CHUNK, 256), 0)
    lc = lax.broadcasted_iota(jnp.int32, (CHUNK, 256), 1) % CHUNK
    mask_tri = rc >= lc
    mask_diag = rc == lc

    def tile4(a):
        return jnp.concatenate([a, a, a, a], axis=0)

    ret1 = []
    ssd1 = []
    late = []
    late_offsets = (OFF_G, OFF_Z, OFF_Z + 256, OFF_LG)
    late_every = max(1, n_chunks // len(late_offsets))
    for c in range(n_chunks):
        rows = slice(c * CHUNK, (c + 1) * CHUNK)
        qc, kc, vc = q[rows], k[rows], v[rows]
        kt4 = tile4(kc.astype(BF16)) * mask_qk
        p_c = (_dot_nt(qc.astype(BF16), kt4) * dmask_ref[...]).astype(BF16)
        kv = _dot_tn((kc * kdec_ref[...]).astype(BF16), vc.astype(BF16)) * mst_ref[...]
        vbd = tile4(vc.astype(BF16)) * mask_bd
        ret1.append((p_c, kv, vbd, (qc * qdec_ref[...]).astype(BF16)))
        per_group = []
        for g in range(SSD_GROUPS):
            lanes = slice(g * SSD_GROUP_WIDTH, (g + 1) * SSD_GROUP_WIDTH)
            nl = slice(g * SSD_STATE, (g + 1) * SSD_STATE)
            cumc = cum[rows, lanes]
            xdtc = xdt[rows, lanes]
            bc = bm[rows, nl].astype(BF16)
            cc = cm[rows, nl]
            cum_row = jnp.sum(jnp.where(mask_diag, cumc, 0.0), axis=0, keepdims=True)
            lmat = jnp.where(mask_tri, jnp.exp(cumc - cum_row), 0.0)
            m_cg = (_dot_nt(cc, tile4(bc)) * lmat).astype(BF16)
            cum_last = cumc[CHUNK - 1:CHUNK, :]
            upd = _dot_tn(bc, (xdtc * jnp.exp(cum_last - cumc)).astype(BF16))
            xbd = tile4(xdtc.astype(BF16)) * mask_bd
            per_group.append((m_cg, upd, xbd, cc, jnp.exp(cumc), jnp.exp(cum_last)))
        ssd1.append(per_group)
        if c % late_every == late_every - 1 and len(late) < len(late_offsets):
            late.append(proj(late_offsets[len(late)], 256))
    while len(late) < len(late_offsets):
        late.append(proj(late_offsets[len(late)], 256))
    g_proj = late[0]
    z_proj = jnp.concatenate(late[1:3], axis=1)
    lg_proj = late[3]

    s_ret = [sret_s[...]]
    s_ssd = [[sssd_s[g] for g in range(SSD_GROUPS)]]
    for c in range(n_chunks):
        s_ret.append(s_ret[c] * cdec_ref[...] + ret1[c][1])
        s_ssd.append([s_ssd[c][g] * ssd1[c][g][5] + ssd1[c][g][1] for g in range(SSD_GROUPS)])
    sret_s[...] = s_ret[n_chunks]
    for g in range(SSD_GROUPS):
        sssd_s[g] = s_ssd[n_chunks][g]

    yr_parts = []
    ys_parts = []
    for c in range(n_chunks):
        p_c, _, vbd, qd = ret1[c]
        yr_parts.append(_dot(p_c, vbd) + _dot(qd, s_ret[c].astype(BF16)))
        row_parts = []
        for g in range(SSD_GROUPS):
            m_cg, _, xbd, cc, ecum, _ = ssd1[c][g]
            row_parts.append(_dot(m_cg, xbd) + ecum * _dot(cc, s_ssd[c][g].astype(BF16)))
        ys_parts.append(jnp.concatenate(row_parts, axis=1))
    yr = jnp.concatenate(yr_parts, axis=0)
    yssd = jnp.concatenate(ys_parts, axis=0)

    gavg = gavg_ref[...]

    def head_mean(a):
        hi = a.astype(BF16)
        lo = (a - hi.astype(F32)).astype(BF16)
        return _dot(hi, gavg) + _dot(lo, gavg)

    yc = yr - head_mean(yr)
    var = head_mean(yc * yc)
    y_ret = yc * lax.rsqrt(var + EPS) * _silu(g_proj)

    ys = (yssd + xs * dsk_ref[...]) * _silu(z_proj)
    parts = []
    for g in range(SSD_GROUPS):
        yg = ys[:, g * SSD_GROUP_WIDTH:(g + 1) * SSD_GROUP_WIDTH]
        parts.append(yg * lax.rsqrt(jnp.mean(yg * yg, axis=-1, keepdims=True) + EPS))
    y_ssd = jnp.concatenate(parts, axis=-1) * snw_ref[...]

    y_lru = h_lru * _gelu_tanh(lg_proj)

    o = _dot(y_ret.astype(BF16), wout_ref[0:256, :])
    o = o + _dot(y_ssd.astype(BF16), wout_ref[256:768, :])
    o = o + _dot(y_lru.astype(BF16), wout_ref[768:1024, :])
    out_ref[...] = x + o


def _const_spec(shape):
    nd = len(shape)
    return pl.BlockSpec(shape, lambda b, j: (0,) * nd, pipeline_mode=pl.Buffered(1))


def _mixer_call(x, tables, p):
    bsz, seq, d = x.shape
    t = TOKENS_MIX
    cos_t, sin_t, dmask_t, qdec_t, kdec_t, cdec_t = tables
    masks = _mixer_masks(min(t, CUMSUM_ROWS))
    xspec = pl.BlockSpec((None, t, d), lambda b, j: (b, j, 0))
    tspec = pl.BlockSpec((t, 128), lambda b, j: (j, 0))
    consts = [p['n1'], p['win'], dmask_t, qdec_t, kdec_t, cdec_t, *masks,
              p['scw'], p['scb'], p['dtb'], p['alog'], p['dsk'], p['snw'],
              p['lcw'], p['lcb'], p['lw'], p['lb'], p['lam'], p['wout']]
    scratch = [
        pltpu.VMEM((256, 256), F32), pltpu.VMEM((SSD_GROUPS, SSD_STATE, SSD_GROUP_WIDTH), F32),
        pltpu.VMEM((SUBLANES, LRU_WIDTH), F32),
        pltpu.VMEM((SUBLANES, SSD_CONV_DIM), F32), pltpu.VMEM((SUBLANES, LRU_WIDTH), F32),
    ]
    return pl.pallas_call(
        _mixer_kernel,
        grid=(bsz, seq // t),
        in_specs=[xspec, tspec, tspec] + [_const_spec(c.shape) for c in consts],
        out_specs=xspec,
        out_shape=jax.ShapeDtypeStruct(x.shape, x.dtype),
        scratch_shapes=scratch,
        compiler_params=pltpu.CompilerParams(
            dimension_semantics=("parallel", "arbitrary"),
            vmem_limit_bytes=VMEM_LIMIT_BYTES),
        name="mixer",
    )(x, cos_t, sin_t, *consts)


def _ffn_kernel(*refs, final_norm):
    n_slabs = D_MODEL // 128
    x_slabs = refs[:n_slabs]
    n2_ref, wup_ref, cw_ref, cb_ref, wdn_ref, fin_ref, out_ref, up_tail, out_slabs = refs[n_slabs:]
    t = out_ref.shape[0]
    n_rows = t // SUBLANES
    j = pl.program_id(1)

    @pl.when(j == 0)
    def _():
        up_tail[...] = jnp.zeros_like(up_tail)

    x = jnp.concatenate(
        [jnp.concatenate([ref[pl.ds(_perm_row_start(kk), SUBLANES, stride=PERM_STRIDE), :]
                          for kk in range(n_rows)], axis=0) for ref in x_slabs], axis=1)
    h = _rmsnorm(x, n2_ref[...]).astype(BF16)
    bounds = list(range(0, D_FF, FFN_COLS)) + [D_FF]
    n_chunks = len(bounds) - 1

    def up_proj(cidx):
        return [_dot(h, wup_ref[:, base + bounds[cidx]:base + bounds[cidx + 1]])
                for base in (0, D_FF)]

    ups = [up_proj(c) for c in range(min(FFN_LOOKAHEAD, n_chunks))]
    acc = x
    for cidx in range(n_chunks):
        if cidx + FFN_LOOKAHEAD < n_chunks:
            ups.append(up_proj(cidx + FFN_LOOKAHEAD))
        halves = []
        for half, base in enumerate((0, D_FF)):
            cols = slice(base + bounds[cidx], base + bounds[cidx + 1])
            halves.append(_causal_conv_perm(ups[cidx][half], up_tail, cw_ref, cb_ref, FFN_CONV, cols))
        ups[cidx] = None
        act = (_gelu_tanh_x2(halves[0]) * halves[1]).astype(BF16)
        acc = acc + _dot(act, wdn_ref[bounds[cidx]:bounds[cidx + 1], :])
    if final_norm:
        acc = _rmsnorm(acc, fin_ref[...])
    for c in range(n_slabs):
        for kk in range(n_rows):
            out_slabs[c, pl.ds(_perm_row_start(kk), SUBLANES, stride=PERM_STRIDE), :] = (
                acc[SUBLANES * kk:SUBLANES * (kk + 1), 128 * c:128 * (c + 1)])
    for c in range(n_slabs):
        out_ref[:, 128 * c:128 * (c + 1)] = out_slabs[c]


def _ffn_call(x, p, final_w, final_norm):
    bsz, seq, d = x.shape
    t = TOKENS_FFN
    n_slabs = d // 128
    xspec = pl.BlockSpec((None, t, d), lambda b, j: (b, j, 0))
    slab_specs = [pl.BlockSpec((None, t, 128), functools.partial(lambda b, j, c: (b, j, c), c=c))
                  for c in range(n_slabs)]
    consts = [p['n2'], p['wup'], p['fcw'], p['fcb'], p['wdn'], final_w]
    return pl.pallas_call(
        functools.partial(_ffn_kernel, final_norm=final_norm),
        grid=(bsz, seq // t),
        in_specs=slab_specs + [_const_spec(c.shape) for c in consts],
        out_specs=xspec,
        out_shape=jax.ShapeDtypeStruct(x.shape, x.dtype),
        scratch_shapes=[pltpu.VMEM((PERM_BLOCK, 2 * D_FF), F32), pltpu.VMEM((n_slabs, t, 128), F32)],
        compiler_params=pltpu.CompilerParams(
            dimension_semantics=("parallel", "arbitrary"),
            vmem_limit_bytes=VMEM_LIMIT_BYTES),
        name="ffn",
    )(*([x] * n_slabs), *consts)


def _mixer_masks(t):
    r = np.arange(t)
    tril = ((r[:, None] >= r[None, :]) & (r[:, None] // CHUNK == r[None, :] // CHUNK))
    rb = np.arange(4 * CHUNK)[:, None] // CHUNK
    ln = np.arange(256)[None, :]
    mask_qk = rb == (ln % 128) // 32
    mask_bd = rb == ln // CHUNK
    mask_state = ((np.arange(256)[:, None] % 128) // 32) == ln // CHUNK
    gavg = (ln // RET_DK == np.arange(256)[:, None] // RET_DK) * (1.0 / RET_DK)
    return (jnp.asarray(tril, BF16), jnp.asarray(mask_qk, BF16), jnp.asarray(mask_bd, BF16),
            jnp.asarray(mask_state, F32), jnp.asarray(gavg, BF16))


def _retention_tables(seq):
    half = RET_DK // 2
    pos = jnp.arange(seq, dtype=F32)
    inv = ROPE_BASE ** (-jnp.arange(half, dtype=F32) / half)
    ang = pos[:, None] * inv[None, :]
    cos_t = jnp.tile(jnp.cos(ang), (1, RET_HEADS))
    sin_t = jnp.tile(jnp.sin(ang), (1, RET_HEADS))
    log_gamma = jnp.log(1.0 - 2.0 ** (-5.0 - jnp.arange(RET_HEADS, dtype=F32)))
    idx = jnp.arange(CHUNK, dtype=F32)
    diff = idx[:, None] - idx[None, :]
    dmask = jnp.where(diff >= 0, jnp.exp(log_gamma[:, None, None] * jnp.maximum(diff, 0.0)), 0.0)
    scale = RET_DK ** -0.5
    dmask_t = jnp.transpose(dmask, (1, 0, 2)).reshape(CHUNK, RET_HEADS * CHUNK) * scale
    qk_head = (np.arange(256) % 128) // 32
    q_decay = jnp.exp(log_gamma[:, None] * (idx + 1.0)[None, :])
    k_decay = jnp.exp(log_gamma[:, None] * (CHUNK - 1 - idx)[None, :])
    qdec_t = q_decay.T[:, qk_head] * scale
    kdec_t = k_decay.T[:, qk_head]
    cdec_t = jnp.exp(log_gamma * CHUNK)[np.arange(256) // RET_DK][None, :]
    return cos_t, sin_t, dmask_t, qdec_t, kdec_t, cdec_t


def _layer_params(l, norm1_w, w_in, ssd_conv_w, ssd_conv_b, ssd_dt_bias, ssd_a_log, ssd_d,
                  ssd_norm_w, lru_conv_w, lru_conv_b, lru_w_a, lru_b_a, lru_w_x, lru_b_x,
                  lru_lambda, w_out, norm2_w, ffn_w_up, ffn_conv_w, ffn_conv_b, ffn_w_down):
    w = w_in[l]
    j = np.arange(256)
    qk_perm = ((j % 128) // 32) * RET_DK + (j // 128) * (RET_DK // 2) + j % 32
    head_of_lane = np.arange(SSD_WIDTH) // (SSD_WIDTH // SSD_HEADS)
    o = 0
    wq = w[:, o:o + 256][:, qk_perm]; o += 256
    wk = w[:, o:o + 256][:, qk_perm]; o += 256
    wv = w[:, o:o + 256]; o += 256
    wg = w[:, o:o + 256]; o += 256
    wz = w[:, o:o + SSD_WIDTH]; o += SSD_WIDTH
    wxbc = w[:, o:o + SSD_CONV_DIM]; o += SSD_CONV_DIM
    wdt = w[:, o:o + SSD_HEADS][:, head_of_lane]; o += SSD_HEADS
    wlg = w[:, o:o + LRU_WIDTH]; o += LRU_WIDTH
    wlx = w[:, o:o + LRU_WIDTH]
    win = jnp.concatenate([wq, wk, wv, wg, wz, wxbc, wdt, wlg, wlx], axis=1).astype(BF16)
    blk = LRU_WIDTH // LRU_BLOCKS
    half_gate = jnp.concatenate([jnp.ones((1, D_FF), F32), jnp.full((1, D_FF), 0.5, F32)], axis=1)

    def block_diag(wb):
        full = jnp.zeros((LRU_WIDTH, LRU_WIDTH), F32)
        for kb in range(LRU_BLOCKS):
            full = full.at[kb * blk:(kb + 1) * blk, kb * blk:(kb + 1) * blk].set(wb[kb])
        return full

    lw = jnp.concatenate([block_diag(lru_w_a[l]), block_diag(lru_w_x[l])], axis=1).astype(BF16)
    lb = jnp.concatenate([lru_b_a[l], lru_b_x[l]])[None, :]
    return dict(
        n1=norm1_w[l][None, :], win=win,
        scw=ssd_conv_w[l], scb=ssd_conv_b[l][None, :],
        dtb=ssd_dt_bias[l][head_of_lane][None, :], alog=ssd_a_log[l][head_of_lane][None, :],
        dsk=ssd_d[l][head_of_lane][None, :], snw=ssd_norm_w[l][None, :],
        lcw=lru_conv_w[l], lcb=lru_conv_b[l][None, :], lw=lw, lb=lb,
        lam=lru_lambda[l][None, :], wout=w_out[l].astype(BF16),
        n2=norm2_w[l][None, :], wup=ffn_w_up[l].astype(BF16), fcw=ffn_conv_w[l] * half_gate,
        fcb=(ffn_conv_b[l] * half_gate[0])[None, :], wdn=ffn_w_down[l].astype(BF16))


def kernel(x, norm1_w, w_in, ssd_conv_w, ssd_conv_b, ssd_dt_bias, ssd_a_log, ssd_d, ssd_norm_w, lru_conv_w, lru_conv_b, lru_w_a, lru_b_a, lru_w_x, lru_b_x, lru_lambda, w_out, norm2_w, ffn_w_up, ffn_conv_w, ffn_conv_b, ffn_w_down, final_norm_w):
    depth = w_in.shape[0]
    seq = x.shape[1]
    assert seq % TOKENS_MIX == 0 and seq % TOKENS_FFN == 0 and x.shape[2] == D_MODEL
    tables = _retention_tables(seq)
    fin = final_norm_w[None, :]
    for l in range(depth):
        p = _layer_params(l, norm1_w, w_in, ssd_conv_w, ssd_conv_b, ssd_dt_bias, ssd_a_log, ssd_d,
                          ssd_norm_w, lru_conv_w, lru_conv_b, lru_w_a, lru_b_a, lru_w_x, lru_b_x,
                          lru_lambda, w_out, norm2_w, ffn_w_up, ffn_conv_w, ffn_conv_b, ffn_w_down)
        x = _mixer_call(x, tables, p)
        x = _ffn_call(x, p, fin, final_norm=(l == depth - 1))
    return x
```

```python
import functools
import math

import jax
import jax.numpy as jnp
import numpy as np
from jax import lax
from jax.experimental import pallas as pl
from jax.experimental.pallas import tpu as pltpu

D_MODEL = 1024
CHUNK = 64
EPS = 1e-6

RET_HEADS = 4
RET_DK = 64
RET_WIDTH = 256
ROPE_BASE = 10000.0

SSD_HEADS = 8
SSD_WIDTH = 512
SSD_GROUPS = 2
SSD_STATE = 128
SSD_CONV = 4
SSD_CONV_DIM = 1024
SSD_GROUP_WIDTH = SSD_WIDTH // SSD_GROUPS

LRU_WIDTH = 256
LRU_BLOCKS = 4
LRU_CONV = 4
LRU_C = 8.0

D_FF = 2816
FFN_CONV = 3

SUBLANES = 8
VMEM_LIMIT_BYTES = 56 * 1024 * 1024

OFF_Q = 0
OFF_K = 256
OFF_V = 512
OFF_G = 768
OFF_Z = 1024
OFF_XBC = 1536
OFF_DT = 2560
OFF_LG = 3072
OFF_LX = 3328
D_PROJ_PAD = 3584

TOKENS_MIX = 512
TOKENS_FFN = 512
FFN_COLS = 768
FFN_LOOKAHEAD = 2
CUMSUM_ROWS = 256

F32 = jnp.float32
BF16 = jnp.bfloat16


def _dot(a, b):
    return jnp.dot(a, b, preferred_element_type=F32)


def _dot_nt(a, b):
    return lax.dot_general(a, b, (((1,), (1,)), ((), ())), preferred_element_type=F32)


def _dot_tn(a, b):
    return lax.dot_general(a, b, (((0,), (0,)), ((), ())), preferred_element_type=F32)


def _sigmoid(x):
    return 1.0 / (1.0 + jnp.exp(-x))


def _silu(x):
    return x * _sigmoid(x)


def _softplus(x):
    return jnp.maximum(x, 0.0) + jnp.log(1.0 + jnp.exp(-jnp.abs(x)))


def _gelu_tanh(x):
    c = math.sqrt(2.0 / math.pi)
    return 0.5 * x * (1.0 + jnp.tanh(c * (x + 0.044715 * (x * x * x))))


def _gelu_tanh_x2(x):
    c = math.sqrt(2.0 / math.pi)
    return x * (1.0 + jnp.tanh(x * (c + (c * 0.044715) * (x * x))))


def _rmsnorm(x, w):
    ms = jnp.mean(x * x, axis=-1, keepdims=True)
    return x * lax.rsqrt(ms + EPS) * w


def _split3_bf16(x):
    hi = x.astype(BF16)
    r1 = x - hi.astype(F32)
    mid = r1.astype(BF16)
    lo = (r1 - mid.astype(F32)).astype(BF16)
    return hi, mid, lo


def _causal_conv(cur, tail_ref, w_ref, b_ref, width, cols=None):
    cs = slice(None) if cols is None else cols
    t = cur.shape[0]
    sub = lax.broadcasted_iota(jnp.int32, (SUBLANES, cur.shape[1]), 0)
    prev = tail_ref[:, cs]
    acc = cur * w_ref[width - 1:width, cs] + b_ref[0:1, cs]
    for back in range(1, width):
        shifted = pltpu.roll(cur, back, axis=0)
        head = jnp.where(sub < back, pltpu.roll(prev, back, axis=0), shifted[:SUBLANES])
        shifted = jnp.concatenate([head, shifted[SUBLANES:]], axis=0)
        acc = acc + shifted * w_ref[width - 1 - back:width - back, cs]
    tail_ref[:, cs] = cur[t - SUBLANES:, :]
    return acc


PERM_STRIDE = 4
PERM_BLOCK = SUBLANES * PERM_STRIDE


def _perm_row_start(vreg_row):
    return PERM_BLOCK * (vreg_row // PERM_STRIDE) + vreg_row % PERM_STRIDE


def _causal_conv_perm(cur, tail_ref, w_ref, b_ref, width, cols):
    t = cur.shape[0]
    n_rows = t // SUBLANES
    sub = lax.broadcasted_iota(jnp.int32, (SUBLANES, cur.shape[1]), 0)
    vrows = [cur[SUBLANES * kk:SUBLANES * (kk + 1)] for kk in range(n_rows)]
    prev = [tail_ref[SUBLANES * kk:SUBLANES * (kk + 1), cols] for kk in range(PERM_STRIDE)]
    rolled = {}

    def down_one(blk, kk):
        if (blk, kk) not in rolled:
            src = prev[kk] if blk < 0 else vrows[blk * PERM_STRIDE + kk]
            rolled[(blk, kk)] = pltpu.roll(src, 1, axis=0)
        return rolled[(blk, kk)]

    fixes = {}

    def fix(blk, kk):
        if (blk, kk) not in fixes:
            fixes[(blk, kk)] = jnp.where(sub == 0, down_one(blk - 1, kk), down_one(blk, kk))
        return fixes[(blk, kk)]

    acc = cur * w_ref[width - 1:width, cols] + b_ref[0:1, cols]
    for back in range(1, width):
        pieces = []
        for kk in range(n_rows):
            blk, off = divmod(kk, PERM_STRIDE)
            pieces.append(vrows[kk - back] if off >= back else fix(blk, off - back + PERM_STRIDE))
        acc = acc + jnp.concatenate(pieces, axis=0) * w_ref[width - 1 - back:width - back, cols]
    tail_ref[:, cols] = cur[t - PERM_BLOCK:, :]
    return acc


def _mixer_kernel(x_ref, cos_ref, sin_ref, n1_ref, win_ref,
                  dmask_ref, qdec_ref, kdec_ref, cdec_ref,
                  tril_ref, mqk_ref, mbd_ref, mst_ref, gavg_ref,
                  scw_ref, scb_ref, dtb_ref, alog_ref, dsk_ref, snw_ref,
                  lcw_ref, lcb_ref, lw_ref, lb_ref, lam_ref, wout_ref,
                  out_ref,
                  sret_s, sssd_s, hlru_s, xbc_buf, lx_buf):
    t = x_ref.shape[0]
    n_chunks = t // CHUNK
    j = pl.program_id(1)

    @pl.when(j == 0)
    def _():
        sret_s[...] = jnp.zeros_like(sret_s)
        sssd_s[...] = jnp.zeros_like(sssd_s)
        hlru_s[...] = jnp.zeros_like(hlru_s)
        xbc_buf[...] = jnp.zeros_like(xbc_buf)
        lx_buf[...] = jnp.zeros_like(lx_buf)

    x = x_ref[...]
    h = _rmsnorm(x, n1_ref[...]).astype(BF16)

    def proj(off, width):
        return _dot(h, win_ref[:, off:off + width])

    lx_proj = proj(OFF_LX, LRU_WIDTH)
    dt_proj = proj(OFF_DT, SSD_WIDTH)
    xbc_proj = proj(OFF_XBC, SSD_CONV_DIM)

    xc = _causal_conv(lx_proj, lx_buf, lcw_ref, lcb_ref, LRU_CONV)
    gates = _dot(xc.astype(BF16), lw_ref[...]) + lb_ref[...]
    r_gate = _sigmoid(gates[:, :LRU_WIDTH])
    i_gate = _sigmoid(gates[:, LRU_WIDTH:])
    log_a = (-LRU_C) * r_gate * _softplus(-lam_ref[...])
    a = jnp.exp(log_a)
    u = jnp.sqrt(-jnp.tanh(log_a) * (1.0 + a * a)) * (i_gate * xc)
    groups = t // SUBLANES
    a3 = a.reshape(groups, SUBLANES, LRU_WIDTH)
    u3 = u.reshape(groups, SUBLANES, LRU_WIDTH)
    sub = lax.broadcasted_iota(jnp.int32, (groups, SUBLANES, LRU_WIDTH), 1)
    step = 1
    while step < SUBLANES:
        keep = sub >= step
        a_sh = jnp.where(keep, pltpu.roll(a3, step, axis=1), 1.0)
        u_sh = jnp.where(keep, pltpu.roll(u3, step, axis=1), 0.0)
        u3 = a3 * u_sh + u3
        a3 = a3 * a_sh
        step *= 2
    carry = hlru_s[...]
    outs = []
    for g in range(groups):
        hg = a3[g] * carry + u3[g]
        outs.append(hg)
        carry = jnp.broadcast_to(hg[SUBLANES - 1:SUBLANES, :], (SUBLANES, LRU_WIDTH))
    hlru_s[...] = carry
    h_lru = jnp.concatenate(outs, axis=0)

    dt = _softplus(dt_proj + dtb_ref[...])
    da = dt * (-jnp.exp(alog_ref[...]))
    d_hi, d_mid, d_lo = _split3_bf16(da)
    xbc = _silu(_causal_conv(xbc_proj, xbc_buf, scw_ref, scb_ref, SSD_CONV))
    xs = xbc[:, :SSD_WIDTH]
    bm = xbc[:, SSD_WIDTH:SSD_WIDTH + SSD_GROUPS * SSD_STATE]
    cm = xbc[:, SSD_WIDTH + SSD_GROUPS * SSD_STATE:].astype(BF16)
    tril = tril_ref[...]
    tb = tril.shape[0]
    cum = jnp.concatenate(
        [sum(_dot(tril, part[r0:r0 + tb]) for part in (d_hi, d_mid, d_lo)) for r0 in range(0, t, tb)],
        axis=0)
    xdt = xs * dt

    cos = cos_ref[...]
    sin = sin_ref[...]

    def rope(p):
        p1, p2 = p[:, :128], p[:, 128:]
        return jnp.concatenate([p1 * cos - p2 * sin, p1 * sin + p2 * cos], axis=-1)

    q = rope(proj(OFF_Q, 256))
    k = rope(proj(OFF_K, 256))
    v = proj(OFF_V, 256)

    mask_qk = mqk_ref[...]
    mask_bd = mbd_ref[...]
    rc = lax.broadcasted_iota(jnp.int32, (CHUNK, 256), 0)
    lc = lax.broadcasted_iota(jnp.int32, (CHUNK, 256), 1) % CHUNK
    mask_tri = rc >= lc
    mask_diag = rc == lc

    def tile4(a):
        return jnp.concatenate([a, a, a, a], axis=0)

    ret1 = []
    ssd1 = []
    late = []
    late_offsets = (OFF_G, OFF_Z, OFF_Z + 256, OFF_LG)
    late_every = max(1, n_chunks // len(late_offsets))
    for c in range(n_chunks):
        rows = slice(c * CHUNK, (c + 1) * CHUNK)
        qc, kc, vc = q[rows], k[rows], v[rows]
        kt4 = tile4(kc.astype(BF16)) * mask_qk
        p_c = (_dot_nt(qc.astype(BF16), kt4) * dmask_ref[...]).astype(BF16)
        kv = _dot_tn((kc * kdec_ref[...]).astype(BF16), vc.astype(BF16)) * mst_ref[...]
        vbd = tile4(vc.astype(BF16)) * mask_bd
        ret1.append((p_c, kv, vbd, (qc * qdec_ref[...]).astype(BF16)))
        per_group = []
        for g in range(SSD_GROUPS):
            lanes = slice(g * SSD_GROUP_WIDTH, (g + 1) * SSD_GROUP_WIDTH)
            nl = slice(g * SSD_STATE, (g + 1) * SSD_STATE)
            cumc = cum[rows, lanes]
            xdtc = xdt[rows, lanes]
            bc = bm[rows, nl].astype(BF16)
            cc = cm[rows, nl]
            cum_row = jnp.sum(jnp.where(mask_diag, cumc, 0.0), axis=0, keepdims=True)
            lmat = jnp.where(mask_tri, jnp.exp(cumc - cum_row), 0.0)
            m_cg = (_dot_nt(cc, tile4(bc)) * lmat).astype(BF16)
            cum_last = cumc[CHUNK - 1:CHUNK, :]
            upd = _dot_tn(bc, (xdtc * jnp.exp(cum_last - cumc)).astype(BF16))
            xbd = tile4(xdtc.astype(BF16)) * mask_bd
            per_group.append((m_cg, upd, xbd, cc, jnp.exp(cumc), jnp.exp(cum_last)))
        ssd1.append(per_group)
        if c % late_every == 0 and len(late) < len(late_offsets):
            late.append(proj(late_offsets[len(late)], 256))
    while len(late) < len(late_offsets):
        late.append(proj(late_offsets[len(late)], 256))
    g_proj = late[0]
    z_proj = jnp.concatenate(late[1:3], axis=1)
    lg_proj = late[3]

    s_ret = [sret_s[...]]
    s_ssd = [[sssd_s[g] for g in range(SSD_GROUPS)]]
    for c in range(n_chunks):
        s_ret.append(s_ret[c] * cdec_ref[...] + ret1[c][1])
        s_ssd.append([s_ssd[c][g] * ssd1[c][g][5] + ssd1[c][g][1] for g in range(SSD_GROUPS)])
    sret_s[...] = s_ret[n_chunks]
    for g in range(SSD_GROUPS):
        sssd_s[g] = s_ssd[n_chunks][g]

    yr_parts = []
    ys_parts = []
    for c in range(n_chunks):
        p_c, _, vbd, qd = ret1[c]
        yr_parts.append(_dot(p_c, vbd) + _dot(qd, s_ret[c].astype(BF16)))
        row_parts = []
        for g in range(SSD_GROUPS):
            m_cg, _, xbd, cc, ecum, _ = ssd1[c][g]
            row_parts.append(_dot(m_cg, xbd) + ecum * _dot(cc, s_ssd[c][g].astype(BF16)))
        ys_parts.append(jnp.concatenate(row_parts, axis=1))
    yr = jnp.concatenate(yr_parts, axis=0)
    yssd = jnp.concatenate(ys_parts, axis=0)

    gavg = gavg_ref[...]

    def head_mean(a):
        hi = a.astype(BF16)
        lo = (a - hi.astype(F32)).astype(BF16)
        return _dot(hi, gavg) + _dot(lo, gavg)

    yc = yr - head_mean(yr)
    var = head_mean(yc * yc)
    y_ret = yc * lax.rsqrt(var + EPS) * _silu(g_proj)

    ys = (yssd + xs * dsk_ref[...]) * _silu(z_proj)
    parts = []
    for g in range(SSD_GROUPS):
        yg = ys[:, g * SSD_GROUP_WIDTH:(g + 1) * SSD_GROUP_WIDTH]
        parts.append(yg * lax.rsqrt(jnp.mean(yg * yg, axis=-1, keepdims=True) + EPS))
    y_ssd = jnp.concatenate(parts, axis=-1) * snw_ref[...]

    y_lru = h_lru * _gelu_tanh(lg_proj)

    o = _dot(y_ret.astype(BF16), wout_ref[0:256, :])
    o = o + _dot(y_ssd.astype(BF16), wout_ref[256:768, :])
    o = o + _dot(y_lru.astype(BF16), wout_ref[768:1024, :])
    out_ref[...] = x + o


def _const_spec(shape):
    nd = len(shape)
    return pl.BlockSpec(shape, lambda b, j: (0,) * nd, pipeline_mode=pl.Buffered(1))


def _mixer_call(x, tables, p):
    bsz, seq, d = x.shape
    t = TOKENS_MIX
    cos_t, sin_t, dmask_t, qdec_t, kdec_t, cdec_t = tables
    masks = _mixer_masks(min(t, CUMSUM_ROWS))
    xspec = pl.BlockSpec((None, t, d), lambda b, j: (b, j, 0))
    tspec = pl.BlockSpec((t, 128), lambda b, j: (j, 0))
    consts = [p['n1'], p['win'], dmask_t, qdec_t, kdec_t, cdec_t, *masks,
              p['scw'], p['scb'], p['dtb'], p['alog'], p['dsk'], p['snw'],
              p['lcw'], p['lcb'], p['lw'], p['lb'], p['lam'], p['wout']]
    scratch = [
        pltpu.VMEM((256, 256), F32), pltpu.VMEM((SSD_GROUPS, SSD_STATE, SSD_GROUP_WIDTH), F32),
        pltpu.VMEM((SUBLANES, LRU_WIDTH), F32),
        pltpu.VMEM((SUBLANES, SSD_CONV_DIM), F32), pltpu.VMEM((SUBLANES, LRU_WIDTH), F32),
    ]
    return pl.pallas_call(
        _mixer_kernel,
        grid=(bsz, seq // t),
        in_specs=[xspec, tspec, tspec] + [_const_spec(c.shape) for c in consts],
        out_specs=xspec,
        out_shape=jax.ShapeDtypeStruct(x.shape, x.dtype),
        scratch_shapes=scratch,
        compiler_params=pltpu.CompilerParams(
            dimension_semantics=("parallel", "arbitrary"),
            vmem_limit_bytes=VMEM_LIMIT_BYTES),
        name="mixer",
    )(x, cos_t, sin_t, *consts)


def _ffn_kernel(*refs, final_norm):
    n_slabs = D_MODEL // 128
    x_slabs = refs[:n_slabs]
    n2_ref, wup_ref, cw_ref, cb_ref, wdn_ref, fin_ref, out_ref, up_tail, out_slabs = refs[n_slabs:]
    t = out_ref.shape[0]
    n_rows = t // SUBLANES
    j = pl.program_id(1)

    @pl.when(j == 0)
    def _():
        up_tail[...] = jnp.zeros_like(up_tail)

    x = jnp.concatenate(
        [jnp.concatenate([ref[pl.ds(_perm_row_start(kk), SUBLANES, stride=PERM_STRIDE), :]
                          for kk in range(n_rows)], axis=0) for ref in x_slabs], axis=1)
    h = _rmsnorm(x, n2_ref[...]).astype(BF16)
    bounds = list(range(0, D_FF, FFN_COLS)) + [D_FF]
    n_chunks = len(bounds) - 1

    def up_proj(cidx):
        return [_dot(h, wup_ref[:, base + bounds[cidx]:base + bounds[cidx + 1]])
                for base in (0, D_FF)]

    ups = [up_proj(c) for c in range(min(FFN_LOOKAHEAD, n_chunks))]
    acc = None
    for cidx in range(n_chunks):
        if cidx + FFN_LOOKAHEAD < n_chunks:
            ups.append(up_proj(cidx + FFN_LOOKAHEAD))
        halves = []
        for half, base in enumerate((0, D_FF)):
            cols = slice(base + bounds[cidx], base + bounds[cidx + 1])
            halves.append(_causal_conv_perm(ups[cidx][half], up_tail, cw_ref, cb_ref, FFN_CONV, cols))
        ups[cidx] = None
        act = (_gelu_tanh_x2(halves[0]) * halves[1]).astype(BF16)
        part = _dot(act, wdn_ref[bounds[cidx]:bounds[cidx + 1], :])
        acc = part if acc is None else acc + part
    acc = acc + jnp.concatenate(
        [jnp.concatenate([ref[pl.ds(_perm_row_start(kk), SUBLANES, stride=PERM_STRIDE), :]
                          for kk in range(n_rows)], axis=0) for ref in x_slabs], axis=1)
    if final_norm:
        acc = _rmsnorm(acc, fin_ref[...])
    for c in range(n_slabs):
        for kk in range(n_rows):
            out_slabs[c, pl.ds(_perm_row_start(kk), SUBLANES, stride=PERM_STRIDE), :] = (
                acc[SUBLANES * kk:SUBLANES * (kk + 1), 128 * c:128 * (c + 1)])
    for c in range(n_slabs):
        out_ref[:, 128 * c:128 * (c + 1)] = out_slabs[c]


def _ffn_call(x, p, final_w, final_norm):
    bsz, seq, d = x.shape
    t = TOKENS_FFN
    n_slabs = d // 128
    xspec = pl.BlockSpec((None, t, d), lambda b, j: (b, j, 0))
    slab_specs = [pl.BlockSpec((None, t, 128), functools.partial(lambda b, j, c: (b, j, c), c=c))
                  for c in range(n_slabs)]
    consts = [p['n2'], p['wup'], p['fcw'], p['fcb'], p['wdn'], final_w]
    return pl.pallas_call(
        functools.partial(_ffn_kernel, final_norm=final_norm),
        grid=(bsz, seq // t),
        in_specs=slab_specs + [_const_spec(c.shape) for c in consts],
        out_specs=xspec,
        out_shape=jax.ShapeDtypeStruct(x.shape, x.dtype),
        scratch_shapes=[pltpu.VMEM((PERM_BLOCK, 2 * D_FF), F32), pltpu.VMEM((n_slabs, t, 128), F32)],
        compiler_params=pltpu.CompilerParams(
            dimension_semantics=("parallel", "arbitrary"),
            vmem_limit_bytes=VMEM_LIMIT_BYTES),
        name="ffn",
    )(*([x] * n_slabs), *consts)


def _mixer_masks(t):
    r = np.arange(t)
    tril = ((r[:, None] >= r[None, :]) & (r[:, None] // CHUNK == r[None, :] // CHUNK))
    rb = np.arange(4 * CHUNK)[:, None] // CHUNK
    ln = np.arange(256)[None, :]
    mask_qk = rb == (ln % 128) // 32
    mask_bd = rb == ln // CHUNK
    mask_state = ((np.arange(256)[:, None] % 128) // 32) == ln // CHUNK
    gavg = (ln // RET_DK == np.arange(256)[:, None] // RET_DK) * (1.0 / RET_DK)
    return (jnp.asarray(tril, BF16), jnp.asarray(mask_qk, BF16), jnp.asarray(mask_bd, BF16),
            jnp.asarray(mask_state, F32), jnp.asarray(gavg, BF16))


def _retention_tables(seq):
    f32 = np.float32
    half = RET_DK // 2
    pos = np.arange(seq, dtype=f32)
    inv = (f32(ROPE_BASE) ** (-np.arange(half, dtype=f32) / f32(half))).astype(f32)
    ang = pos[:, None] * inv[None, :]
    cos_t = np.tile(np.cos(ang).astype(f32), (1, RET_HEADS))
    sin_t = np.tile(np.sin(ang).astype(f32), (1, RET_HEADS))
    log_gamma = np.log(f32(1.0) - f32(2.0) ** (-f32(5.0) - np.arange(RET_HEADS, dtype=f32))).astype(f32)
    idx = np.arange(CHUNK, dtype=f32)
    diff = idx[:, None] - idx[None, :]
    dmask = np.where(diff >= 0, np.exp(log_gamma[:, None, None] * np.maximum(diff, f32(0.0))), f32(0.0))
    scale = f32(RET_DK ** -0.5)
    dmask_t = (np.transpose(dmask, (1, 0, 2)).reshape(CHUNK, RET_HEADS * CHUNK) * scale).astype(f32)
    qk_head = (np.arange(256) % 128) // 32
    q_decay = np.exp(log_gamma[:, None] * (idx + f32(1.0))[None, :]).astype(f32)
    k_decay = np.exp(log_gamma[:, None] * (f32(CHUNK - 1) - idx)[None, :]).astype(f32)
    qdec_t = (q_decay.T[:, qk_head] * scale).astype(f32)
    kdec_t = k_decay.T[:, qk_head]
    cdec_t = np.exp(log_gamma * f32(CHUNK)).astype(f32)[np.arange(256) // RET_DK][None, :]
    return tuple(jnp.asarray(a, F32) for a in (cos_t, sin_t, dmask_t, qdec_t, kdec_t, cdec_t))


def _layer_params(l, norm1_w, w_in, ssd_conv_w, ssd_conv_b, ssd_dt_bias, ssd_a_log, ssd_d,
                  ssd_norm_w, lru_conv_w, lru_conv_b, lru_w_a, lru_b_a, lru_w_x, lru_b_x,
                  lru_lambda, w_out, norm2_w, ffn_w_up, ffn_conv_w, ffn_conv_b, ffn_w_down):
    w = w_in[l]
    lanes_per_head = SSD_WIDTH // SSD_HEADS

    def halves_first(wc):
        return wc.reshape(-1, RET_HEADS, 2, RET_DK // 2).transpose(0, 2, 1, 3).reshape(-1, 256)

    o = 0
    wq = halves_first(w[:, o:o + 256]); o += 256
    wk = halves_first(w[:, o:o + 256]); o += 256
    wv = w[:, o:o + 256]; o += 256
    wg = w[:, o:o + 256]; o += 256
    wz = w[:, o:o + SSD_WIDTH]; o += SSD_WIDTH
    wxbc = w[:, o:o + SSD_CONV_DIM]; o += SSD_CONV_DIM
    wdt = jnp.repeat(w[:, o:o + SSD_HEADS], lanes_per_head, axis=1); o += SSD_HEADS
    wlg = w[:, o:o + LRU_WIDTH]; o += LRU_WIDTH
    wlx = w[:, o:o + LRU_WIDTH]
    win = jnp.concatenate([wq, wk, wv, wg, wz, wxbc, wdt, wlg, wlx], axis=1).astype(BF16)
    half_gate = jnp.concatenate([jnp.ones((1, D_FF), F32), jnp.full((1, D_FF), 0.5, F32)], axis=1)

    eye = np.eye(LRU_BLOCKS, dtype=bool)

    def block_diag(wb):
        return jnp.where(eye[:, None, :, None], wb[:, :, None, :], 0.0).reshape(LRU_WIDTH, LRU_WIDTH)

    lw = jnp.concatenate([block_diag(lru_w_a[l]), block_diag(lru_w_x[l])], axis=1).astype(BF16)
    lb = jnp.concatenate([lru_b_a[l], lru_b_x[l]])[None, :]
    return dict(
        n1=norm1_w[l][None, :], win=win,
        scw=ssd_conv_w[l], scb=ssd_conv_b[l][None, :],
        dtb=jnp.repeat(ssd_dt_bias[l], lanes_per_head)[None, :],
        alog=jnp.repeat(ssd_a_log[l], lanes_per_head)[None, :],
        dsk=jnp.repeat(ssd_d[l], lanes_per_head)[None, :], snw=ssd_norm_w[l][None, :],
        lcw=lru_conv_w[l], lcb=lru_conv_b[l][None, :], lw=lw, lb=lb,
        lam=lru_lambda[l][None, :], wout=w_out[l].astype(BF16),
        n2=norm2_w[l][None, :], wup=ffn_w_up[l].astype(BF16), fcw=ffn_conv_w[l] * half_gate,
        fcb=(ffn_conv_b[l] * half_gate[0])[None, :], wdn=ffn_w_down[l].astype(BF16))


def kernel(x, norm1_w, w_in, ssd_conv_w, ssd_conv_b, ssd_dt_bias, ssd_a_log, ssd_d, ssd_norm_w, lru_conv_w, lru_conv_b, lru_w_a, lru_b_a, lru_w_x, lru_b_x, lru_lambda, w_out, norm2_w, ffn_w_up, ffn_conv_w, ffn_conv_b, ffn_w_down, final_norm_w):
    depth = w_in.shape[0]
    seq = x.shape[1]
    assert seq % TOKENS_MIX == 0 and seq % TOKENS_FFN == 0 and x.shape[2] == D_MODEL
    tables = _retention_tables(seq)
    fin = final_norm_w[None, :]
    for l in range(depth):
        p = _layer_params(l, norm1_w, w_in, ssd_conv_w, ssd_conv_b, ssd_dt_bias, ssd_a_log, ssd_d,
                          ssd_norm_w, lru_conv_w, lru_conv_b, lru_w_a, lru_b_a, lru_w_x, lru_b_x,
                          lru_lambda, w_out, norm2_w, ffn_w_up, ffn_conv_w, ffn_conv_b, ffn_w_down)
        x = _mixer_call(x, tables, p)
        x = _ffn_call(x, p, fin, final_norm=(l == depth - 1))
    return x
```

```python
import functools
import math

import jax
import jax.numpy as jnp
import numpy as np
from jax import lax
from jax.experimental import pallas as pl
from jax.experimental.pallas import tpu as pltpu

D_MODEL = 1024
CHUNK = 64
EPS = 1e-6

RET_HEADS = 4
RET_DK = 64
RET_WIDTH = 256
ROPE_BASE = 10000.0

SSD_HEADS = 8
SSD_WIDTH = 512
SSD_GROUPS = 2
SSD_STATE = 128
SSD_CONV = 4
SSD_CONV_DIM = 1024
SSD_GROUP_WIDTH = SSD_WIDTH // SSD_GROUPS

LRU_WIDTH = 256
LRU_BLOCKS = 4
LRU_CONV = 4
LRU_C = 8.0

D_FF = 2816
FFN_CONV = 3

SUBLANES = 8
VMEM_LIMIT_BYTES = 56 * 1024 * 1024

OFF_Q = 0
OFF_K = 256
OFF_V = 512
OFF_G = 768
OFF_Z = 1024
OFF_XBC = 1536
OFF_DT = 2560
OFF_LG = 3072
OFF_LX = 3328
D_PROJ_PAD = 3584

TOKENS_MIX = 1024
TOKENS_FFN = 512
FFN_COLS = 768
FFN_LOOKAHEAD = 2
MIX_BLOCKS = 2
MIX_SKEW = 4
CUMSUM_ROWS = 256

F32 = jnp.float32
BF16 = jnp.bfloat16


def _dot(a, b):
    return jnp.dot(a, b, preferred_element_type=F32)


def _dot_nt(a, b):
    return lax.dot_general(a, b, (((1,), (1,)), ((), ())), preferred_element_type=F32)


def _dot_tn(a, b):
    return lax.dot_general(a, b, (((0,), (0,)), ((), ())), preferred_element_type=F32)


def _sigmoid(x):
    return 1.0 / (1.0 + jnp.exp(-x))


def _silu(x):
    return x * _sigmoid(x)


def _softplus(x):
    return jnp.maximum(x, 0.0) + jnp.log(1.0 + jnp.exp(-jnp.abs(x)))


def _gelu_tanh(x):
    c = math.sqrt(2.0 / math.pi)
    return 0.5 * x * (1.0 + jnp.tanh(c * (x + 0.044715 * (x * x * x))))


def _gelu_tanh_x2(x):
    c = math.sqrt(2.0 / math.pi)
    return x * (1.0 + jnp.tanh(x * (c + (c * 0.044715) * (x * x))))


def _rmsnorm(x, w):
    ms = jnp.mean(x * x, axis=-1, keepdims=True)
    return x * lax.rsqrt(ms + EPS) * w


def _split3_bf16(x):
    hi = x.astype(BF16)
    r1 = x - hi.astype(F32)
    mid = r1.astype(BF16)
    lo = (r1 - mid.astype(F32)).astype(BF16)
    return hi, mid, lo


def _causal_conv(cur, tail_ref, w_ref, b_ref, width, cols=None):
    cs = slice(None) if cols is None else cols
    t = cur.shape[0]
    sub = lax.broadcasted_iota(jnp.int32, (SUBLANES, cur.shape[1]), 0)
    prev = tail_ref[:, cs]
    acc = cur * w_ref[width - 1:width, cs] + b_ref[0:1, cs]
    for back in range(1, width):
        shifted = pltpu.roll(cur, back, axis=0)
        head = jnp.where(sub < back, pltpu.roll(prev, back, axis=0), shifted[:SUBLANES])
        shifted = jnp.concatenate([head, shifted[SUBLANES:]], axis=0)
        acc = acc + shifted * w_ref[width - 1 - back:width - back, cs]
    tail_ref[:, cs] = cur[t - SUBLANES:, :]
    return acc


PERM_STRIDE = 4
PERM_BLOCK = SUBLANES * PERM_STRIDE


def _perm_row_start(vreg_row):
    return PERM_BLOCK * (vreg_row // PERM_STRIDE) + vreg_row % PERM_STRIDE


def _causal_conv_perm(cur, tail_ref, w_ref, b_ref, width, cols):
    t = cur.shape[0]
    n_rows = t // SUBLANES
    sub = lax.broadcasted_iota(jnp.int32, (SUBLANES, cur.shape[1]), 0)
    vrows = [cur[SUBLANES * kk:SUBLANES * (kk + 1)] for kk in range(n_rows)]
    prev = [tail_ref[SUBLANES * kk:SUBLANES * (kk + 1), cols] for kk in range(PERM_STRIDE)]
    rolled = {}

    def down_one(blk, kk):
        if (blk, kk) not in rolled:
            src = prev[kk] if blk < 0 else vrows[blk * PERM_STRIDE + kk]
            rolled[(blk, kk)] = pltpu.roll(src, 1, axis=0)
        return rolled[(blk, kk)]

    fixes = {}

    def fix(blk, kk):
        if (blk, kk) not in fixes:
            fixes[(blk, kk)] = jnp.where(sub == 0, down_one(blk - 1, kk), down_one(blk, kk))
        return fixes[(blk, kk)]

    acc = cur * w_ref[width - 1:width, cols] + b_ref[0:1, cols]
    for back in range(1, width):
        pieces = []
        for kk in range(n_rows):
            blk, off = divmod(kk, PERM_STRIDE)
            pieces.append(vrows[kk - back] if off >= back else fix(blk, off - back + PERM_STRIDE))
        acc = acc + jnp.concatenate(pieces, axis=0) * w_ref[width - 1 - back:width - back, cols]
    tail_ref[:, cols] = cur[t - PERM_BLOCK:, :]
    return acc


def _mixer_rows(r0, t, x_ref, cos_ref, sin_ref, n1_ref, win_ref,
                dmask_ref, qdec_ref, kdec_ref, cdec_ref,
                tril_ref, mqk_ref, mbd_ref, mst_ref, gavg_ref,
                scw_ref, scb_ref, dtb_ref, alog_ref, dsk_ref, snw_ref,
                lcw_ref, lcb_ref, lw_ref, lb_ref, lam_ref, wout_ref,
                out_ref,
                sret_s, sssd_s, hlru_s, xbc_buf, lx_buf):
    n_chunks = t // CHUNK
    rsl = pl.ds(r0, t)
    x = x_ref[rsl, :]
    h = _rmsnorm(x, n1_ref[...]).astype(BF16)

    def proj(off, width):
        return _dot(h, win_ref[:, off:off + width])

    lx_proj = proj(OFF_LX, LRU_WIDTH)
    dt_proj = proj(OFF_DT, SSD_WIDTH)
    xbc_proj = proj(OFF_XBC, SSD_CONV_DIM)

    yield
    xc = _causal_conv(lx_proj, lx_buf, lcw_ref, lcb_ref, LRU_CONV)
    gates = _dot(xc.astype(BF16), lw_ref[...]) + lb_ref[...]
    r_gate = _sigmoid(gates[:, :LRU_WIDTH])
    i_gate = _sigmoid(gates[:, LRU_WIDTH:])
    log_a = (-LRU_C) * r_gate * _softplus(-lam_ref[...])
    a = jnp.exp(log_a)
    u = jnp.sqrt(-jnp.tanh(log_a) * (1.0 + a * a)) * (i_gate * xc)
    groups = t // SUBLANES
    a3 = a.reshape(groups, SUBLANES, LRU_WIDTH)
    u3 = u.reshape(groups, SUBLANES, LRU_WIDTH)
    sub = lax.broadcasted_iota(jnp.int32, (groups, SUBLANES, LRU_WIDTH), 1)
    step = 1
    while step < SUBLANES:
        keep = sub >= step
        a_sh = jnp.where(keep, pltpu.roll(a3, step, axis=1), 1.0)
        u_sh = jnp.where(keep, pltpu.roll(u3, step, axis=1), 0.0)
        u3 = a3 * u_sh + u3
        a3 = a3 * a_sh
        step *= 2
    carry = hlru_s[...]
    outs = []
    for g in range(groups):
        hg = a3[g] * carry + u3[g]
        outs.append(hg)
        carry = jnp.broadcast_to(hg[SUBLANES - 1:SUBLANES, :], (SUBLANES, LRU_WIDTH))
    hlru_s[...] = carry
    h_lru = jnp.concatenate(outs, axis=0)

    yield
    dt = _softplus(dt_proj + dtb_ref[...])
    da = dt * (-jnp.exp(alog_ref[...]))
    d_hi, d_mid, d_lo = _split3_bf16(da)
    xbc = _silu(_causal_conv(xbc_proj, xbc_buf, scw_ref, scb_ref, SSD_CONV))
    xs = xbc[:, :SSD_WIDTH]
    bm = xbc[:, SSD_WIDTH:SSD_WIDTH + SSD_GROUPS * SSD_STATE]
    cm = xbc[:, SSD_WIDTH + SSD_GROUPS * SSD_STATE:].astype(BF16)
    tril = tril_ref[...]
    tb = tril.shape[0]
    cum = jnp.concatenate(
        [sum(_dot(tril, part[r0:r0 + tb]) for part in (d_hi, d_mid, d_lo)) for r0 in range(0, t, tb)],
        axis=0)
    xdt = xs * dt

    yield
    cos = cos_ref[rsl, :]
    sin = sin_ref[rsl, :]

    def rope(p):
        p1, p2 = p[:, :128], p[:, 128:]
        return jnp.concatenate([p1 * cos - p2 * sin, p1 * sin + p2 * cos], axis=-1)

    q = rope(proj(OFF_Q, 256))
    k = rope(proj(OFF_K, 256))
    v = proj(OFF_V, 256)

    mask_qk = mqk_ref[...]
    mask_bd = mbd_ref[...]
    rc = lax.broadcasted_iota(jnp.int32, (CHUNK, 256), 0)
    lc = lax.broadcasted_iota(jnp.int32, (CHUNK, 256), 1) % CHUNK
    mask_tri = rc >= lc
    mask_diag = rc == lc

    def tile4(a):
        return jnp.concatenate([a, a, a, a], axis=0)

    ret1 = []
    ssd1 = []
    late = []
    late_offsets = (OFF_G, OFF_Z, OFF_Z + 256, OFF_LG)
    late_every = max(1, n_chunks // len(late_offsets))
    for c in range(n_chunks):
        rows = slice(c * CHUNK, (c + 1) * CHUNK)
        qc, kc, vc = q[rows], k[rows], v[rows]
        kt4 = tile4(kc.astype(BF16)) * mask_qk
        p_c = (_dot_nt(qc.astype(BF16), kt4) * dmask_ref[...]).astype(BF16)
        kv = _dot_tn((kc * kdec_ref[...]).astype(BF16), vc.astype(BF16)) * mst_ref[...]
        vbd = tile4(vc.astype(BF16)) * mask_bd
        ret1.append((p_c, kv, vbd, (qc * qdec_ref[...]).astype(BF16)))
        per_group = []
        for g in range(SSD_GROUPS):
            lanes = slice(g * SSD_GROUP_WIDTH, (g + 1) * SSD_GROUP_WIDTH)
            nl = slice(g * SSD_STATE, (g + 1) * SSD_STATE)
            cumc = cum[rows, lanes]
            xdtc = xdt[rows, lanes]
            bc = bm[rows, nl].astype(BF16)
            cc = cm[rows, nl]
            cum_row = jnp.sum(jnp.where(mask_diag, cumc, 0.0), axis=0, keepdims=True)
            lmat = jnp.where(mask_tri, jnp.exp(cumc - cum_row), 0.0)
            m_cg = (_dot_nt(cc, tile4(bc)) * lmat).astype(BF16)
            cum_last = cumc[CHUNK - 1:CHUNK, :]
            upd = _dot_tn(bc, (xdtc * jnp.exp(cum_last - cumc)).astype(BF16))
            xbd = tile4(xdtc.astype(BF16)) * mask_bd
            per_group.append((m_cg, upd, xbd, cc, jnp.exp(cumc), jnp.exp(cum_last)))
        ssd1.append(per_group)
        if c % late_every == 0 and len(late) < len(late_offsets):
            late.append(proj(late_offsets[len(late)], 256))
    while len(late) < len(late_offsets):
        late.append(proj(late_offsets[len(late)], 256))
    g_proj = late[0]
    z_proj = jnp.concatenate(late[1:3], axis=1)
    lg_proj = late[3]

    yield
    s_ret = [sret_s[...]]
    s_ssd = [[sssd_s[g] for g in range(SSD_GROUPS)]]
    for c in range(n_chunks):
        s_ret.append(s_ret[c] * cdec_ref[...] + ret1[c][1])
        s_ssd.append([s_ssd[c][g] * ssd1[c][g][5] + ssd1[c][g][1] for g in range(SSD_GROUPS)])
    sret_s[...] = s_ret[n_chunks]
    for g in range(SSD_GROUPS):
        sssd_s[g] = s_ssd[n_chunks][g]

    yield
    yr_parts = []
    ys_parts = []
    for c in range(n_chunks):
        p_c, _, vbd, qd = ret1[c]
        yr_parts.append(_dot(p_c, vbd) + _dot(qd, s_ret[c].astype(BF16)))
        row_parts = []
        for g in range(SSD_GROUPS):
            m_cg, _, xbd, cc, ecum, _ = ssd1[c][g]
            row_parts.append(_dot(m_cg, xbd) + ecum * _dot(cc, s_ssd[c][g].astype(BF16)))
        ys_parts.append(jnp.concatenate(row_parts, axis=1))
    yr = jnp.concatenate(yr_parts, axis=0)
    yssd = jnp.concatenate(ys_parts, axis=0)

    yield
    gavg = gavg_ref[...]

    def head_mean(a):
        hi = a.astype(BF16)
        lo = (a - hi.astype(F32)).astype(BF16)
        return _dot(hi, gavg) + _dot(lo, gavg)

    yc = yr - head_mean(yr)
    var = head_mean(yc * yc)
    y_ret = yc * lax.rsqrt(var + EPS) * _silu(g_proj)

    ys = (yssd + xs * dsk_ref[...]) * _silu(z_proj)
    parts = []
    for g in range(SSD_GROUPS):
        yg = ys[:, g * SSD_GROUP_WIDTH:(g + 1) * SSD_GROUP_WIDTH]
        parts.append(yg * lax.rsqrt(jnp.mean(yg * yg, axis=-1, keepdims=True) + EPS))
    y_ssd = jnp.concatenate(parts, axis=-1) * snw_ref[...]

    y_lru = h_lru * _gelu_tanh(lg_proj)

    yield
    o = _dot(y_ret.astype(BF16), wout_ref[0:256, :])
    o = o + _dot(y_ssd.astype(BF16), wout_ref[256:768, :])
    o = o + _dot(y_lru.astype(BF16), wout_ref[768:1024, :])
    out_ref[rsl, :] = x + o
    yield


def _mixer_kernel(x_ref, cos_ref, sin_ref, n1_ref, win_ref,
                  dmask_ref, qdec_ref, kdec_ref, cdec_ref,
                  tril_ref, mqk_ref, mbd_ref, mst_ref, gavg_ref,
                  scw_ref, scb_ref, dtb_ref, alog_ref, dsk_ref, snw_ref,
                  lcw_ref, lcb_ref, lw_ref, lb_ref, lam_ref, wout_ref,
                  out_ref,
                  sret_s, sssd_s, hlru_s, xbc_buf, lx_buf):
    t = x_ref.shape[0]
    j = pl.program_id(1)

    @pl.when(j == 0)
    def _():
        sret_s[...] = jnp.zeros_like(sret_s)
        sssd_s[...] = jnp.zeros_like(sssd_s)
        hlru_s[...] = jnp.zeros_like(hlru_s)
        xbc_buf[...] = jnp.zeros_like(xbc_buf)
        lx_buf[...] = jnp.zeros_like(lx_buf)

    args = (x_ref, cos_ref, sin_ref, n1_ref, win_ref, dmask_ref, qdec_ref, kdec_ref, cdec_ref,
            tril_ref, mqk_ref, mbd_ref, mst_ref, gavg_ref, scw_ref, scb_ref, dtb_ref, alog_ref,
            dsk_ref, snw_ref, lcw_ref, lcb_ref, lw_ref, lb_ref, lam_ref, wout_ref, out_ref,
            sret_s, sssd_s, hlru_s, xbc_buf, lx_buf)
    rows = t // MIX_BLOCKS
    gens = [_mixer_rows(i * rows, rows, *args) for i in range(MIX_BLOCKS)]
    started = 0
    live = []
    tick = 0
    while started < MIX_BLOCKS or live:
        if started < MIX_BLOCKS and tick % MIX_SKEW == 0:
            live.append(gens[started]); started += 1
        for g in list(live):
            try:
                next(g)
            except StopIteration:
                live.remove(g)
        tick += 1


def _const_spec(shape):
    nd = len(shape)
    return pl.BlockSpec(shape, lambda b, j: (0,) * nd, pipeline_mode=pl.Buffered(1))


def _mixer_call(x, tables, p):
    bsz, seq, d = x.shape
    t = TOKENS_MIX
    cos_t, sin_t, dmask_t, qdec_t, kdec_t, cdec_t = tables
    masks = _mixer_masks(min(t, CUMSUM_ROWS))
    xspec = pl.BlockSpec((None, t, d), lambda b, j: (b, j, 0))
    tspec = pl.BlockSpec((t, 128), lambda b, j: (j, 0))
    consts = [p['n1'], p['win'], dmask_t, qdec_t, kdec_t, cdec_t, *masks,
              p['scw'], p['scb'], p['dtb'], p['alog'], p['dsk'], p['snw'],
              p['lcw'], p['lcb'], p['lw'], p['lb'], p['lam'], p['wout']]
    scratch = [
        pltpu.VMEM((256, 256), F32), pltpu.VMEM((SSD_GROUPS, SSD_STATE, SSD_GROUP_WIDTH), F32),
        pltpu.VMEM((SUBLANES, LRU_WIDTH), F32),
        pltpu.VMEM((SUBLANES, SSD_CONV_DIM), F32), pltpu.VMEM((SUBLANES, LRU_WIDTH), F32),
    ]
    return pl.pallas_call(
        _mixer_kernel,
        grid=(bsz, seq // t),
        in_specs=[xspec, tspec, tspec] + [_const_spec(c.shape) for c in consts],
        out_specs=xspec,
        out_shape=jax.ShapeDtypeStruct(x.shape, x.dtype),
        scratch_shapes=scratch,
        compiler_params=pltpu.CompilerParams(
            dimension_semantics=("parallel", "arbitrary"),
            vmem_limit_bytes=VMEM_LIMIT_BYTES),
        name="mixer",
    )(x, cos_t, sin_t, *consts)


def _ffn_kernel(*refs, final_norm):
    n_slabs = D_MODEL // 128
    x_slabs = refs[:n_slabs]
    n2_ref, wup_ref, cw_ref, cb_ref, wdn_ref, fin_ref, out_ref, up_tail, out_slabs = refs[n_slabs:]
    t = out_ref.shape[0]
    n_rows = t // SUBLANES
    j = pl.program_id(1)

    @pl.when(j == 0)
    def _():
        up_tail[...] = jnp.zeros_like(up_tail)

    x = jnp.concatenate(
        [jnp.concatenate([ref[pl.ds(_perm_row_start(kk), SUBLANES, stride=PERM_STRIDE), :]
                          for kk in range(n_rows)], axis=0) for ref in x_slabs], axis=1)
    h = _rmsnorm(x, n2_ref[...]).astype(BF16)
    bounds = list(range(0, D_FF, FFN_COLS)) + [D_FF]
    n_chunks = len(bounds) - 1

    def up_proj(cidx):
        return [_dot(h, wup_ref[:, base + bounds[cidx]:base + bounds[cidx + 1]])
                for base in (0, D_FF)]

    ups = [up_proj(c) for c in range(min(FFN_LOOKAHEAD, n_chunks))]
    acc = None
    for cidx in range(n_chunks):
        if cidx + FFN_LOOKAHEAD < n_chunks:
            ups.append(up_proj(cidx + FFN_LOOKAHEAD))
        halves = []
        for half, base in enumerate((0, D_FF)):
            cols = slice(base + bounds[cidx], base + bounds[cidx + 1])
            halves.append(_causal_conv_perm(ups[cidx][half], up_tail, cw_ref, cb_ref, FFN_CONV, cols))
        ups[cidx] = None
        act = (_gelu_tanh_x2(halves[0]) * halves[1]).astype(BF16)
        part = _dot(act, wdn_ref[bounds[cidx]:bounds[cidx + 1], :])
        acc = part if acc is None else acc + part
    acc = acc + jnp.concatenate(
        [jnp.concatenate([ref[pl.ds(_perm_row_start(kk), SUBLANES, stride=PERM_STRIDE), :]
                          for kk in range(n_rows)], axis=0) for ref in x_slabs], axis=1)
    if final_norm:
        acc = _rmsnorm(acc, fin_ref[...])
    for c in range(n_slabs):
        for kk in range(n_rows):
            out_slabs[c, pl.ds(_perm_row_start(kk), SUBLANES, stride=PERM_STRIDE), :] = (
                acc[SUBLANES * kk:SUBLANES * (kk + 1), 128 * c:128 * (c + 1)])
    for c in range(n_slabs):
        out_ref[:, 128 * c:128 * (c + 1)] = out_slabs[c]


def _ffn_call(x, p, final_w, final_norm):
    bsz, seq, d = x.shape
    t = TOKENS_FFN
    n_slabs = d // 128
    xspec = pl.BlockSpec((None, t, d), lambda b, j: (b, j, 0))
    slab_specs = [pl.BlockSpec((None, t, 128), functools.partial(lambda b, j, c: (b, j, c), c=c))
                  for c in range(n_slabs)]
    consts = [p['n2'], p['wup'], p['fcw'], p['fcb'], p['wdn'], final_w]
    return pl.pallas_call(
        functools.partial(_ffn_kernel, final_norm=final_norm),
        grid=(bsz, seq // t),
        in_specs=slab_specs + [_const_spec(c.shape) for c in consts],
        out_specs=xspec,
        out_shape=jax.ShapeDtypeStruct(x.shape, x.dtype),
        scratch_shapes=[pltpu.VMEM((PERM_BLOCK, 2 * D_FF), F32), pltpu.VMEM((n_slabs, t, 128), F32)],
        compiler_params=pltpu.CompilerParams(
            dimension_semantics=("parallel", "arbitrary"),
            vmem_limit_bytes=VMEM_LIMIT_BYTES),
        name="ffn",
    )(*([x] * n_slabs), *consts)


def _mixer_masks(t):
    r = np.arange(t)
    tril = ((r[:, None] >= r[None, :]) & (r[:, None] // CHUNK == r[None, :] // CHUNK))
    rb = np.arange(4 * CHUNK)[:, None] // CHUNK
    ln = np.arange(256)[None, :]
    mask_qk = rb == (ln % 128) // 32
    mask_bd = rb == ln // CHUNK
    mask_state = ((np.arange(256)[:, None] % 128) // 32) == ln // CHUNK
    gavg = (ln // RET_DK == np.arange(256)[:, None] // RET_DK) * (1.0 / RET_DK)
    return (jnp.asarray(tril, BF16), jnp.asarray(mask_qk, BF16), jnp.asarray(mask_bd, BF16),
            jnp.asarray(mask_state, F32), jnp.asarray(gavg, BF16))


def _retention_tables(seq):
    f32 = np.float32
    half = RET_DK // 2
    pos = np.arange(seq, dtype=f32)
    inv = (f32(ROPE_BASE) ** (-np.arange(half, dtype=f32) / f32(half))).astype(f32)
    ang = pos[:, None] * inv[None, :]
    cos_t = np.tile(np.cos(ang).astype(f32), (1, RET_HEADS))
    sin_t = np.tile(np.sin(ang).astype(f32), (1, RET_HEADS))
    log_gamma = np.log(f32(1.0) - f32(2.0) ** (-f32(5.0) - np.arange(RET_HEADS, dtype=f32))).astype(f32)
    idx = np.arange(CHUNK, dtype=f32)
    diff = idx[:, None] - idx[None, :]
    dmask = np.where(diff >= 0, np.exp(log_gamma[:, None, None] * np.maximum(diff, f32(0.0))), f32(0.0))
    scale = f32(RET_DK ** -0.5)
    dmask_t = (np.transpose(dmask, (1, 0, 2)).reshape(CHUNK, RET_HEADS * CHUNK) * scale).astype(f32)
    qk_head = (np.arange(256) % 128) // 32
    q_decay = np.exp(log_gamma[:, None] * (idx + f32(1.0))[None, :]).astype(f32)
    k_decay = np.exp(log_gamma[:, None] * (f32(CHUNK - 1) - idx)[None, :]).astype(f32)
    qdec_t = (q_decay.T[:, qk_head] * scale).astype(f32)
    kdec_t = k_decay.T[:, qk_head]
    cdec_t = np.exp(log_gamma * f32(CHUNK)).astype(f32)[np.arange(256) // RET_DK][None, :]
    return tuple(jnp.asarray(a, F32) for a in (cos_t, sin_t, dmask_t, qdec_t, kdec_t, cdec_t))


def _layer_params(l, norm1_w, w_in, ssd_conv_w, ssd_conv_b, ssd_dt_bias, ssd_a_log, ssd_d,
                  ssd_norm_w, lru_conv_w, lru_conv_b, lru_w_a, lru_b_a, lru_w_x, lru_b_x,
                  lru_lambda, w_out, norm2_w, ffn_w_up, ffn_conv_w, ffn_conv_b, ffn_w_down):
    w = w_in[l]
    lanes_per_head = SSD_WIDTH // SSD_HEADS

    def halves_first(wc):
        return wc.reshape(-1, RET_HEADS, 2, RET_DK // 2).transpose(0, 2, 1, 3).reshape(-1, 256)

    o = 0
    wq = halves_first(w[:, o:o + 256]); o += 256
    wk = halves_first(w[:, o:o + 256]); o += 256
    wv = w[:, o:o + 256]; o += 256
    wg = w[:, o:o + 256]; o += 256
    wz = w[:, o:o + SSD_WIDTH]; o += SSD_WIDTH
    wxbc = w[:, o:o + SSD_CONV_DIM]; o += SSD_CONV_DIM
    wdt = jnp.repeat(w[:, o:o + SSD_HEADS], lanes_per_head, axis=1); o += SSD_HEADS
    wlg = w[:, o:o + LRU_WIDTH]; o += LRU_WIDTH
    wlx = w[:, o:o + LRU_WIDTH]
    win = jnp.concatenate([wq, wk, wv, wg, wz, wxbc, wdt, wlg, wlx], axis=1).astype(BF16)
    half_gate = jnp.concatenate([jnp.ones((1, D_FF), F32), jnp.full((1, D_FF), 0.5, F32)], axis=1)

    eye = np.eye(LRU_BLOCKS, dtype=bool)

    def block_diag(wb):
        return jnp.where(eye[:, None, :, None], wb[:, :, None, :], 0.0).reshape(LRU_WIDTH, LRU_WIDTH)

    lw = jnp.concatenate([block_diag(lru_w_a[l]), block_diag(lru_w_x[l])], axis=1).astype(BF16)
    lb = jnp.concatenate([lru_b_a[l], lru_b_x[l]])[None, :]
    return dict(
        n1=norm1_w[l][None, :], win=win,
        scw=ssd_conv_w[l], scb=ssd_conv_b[l][None, :],
        dtb=jnp.repeat(ssd_dt_bias[l], lanes_per_head)[None, :],
        alog=jnp.repeat(ssd_a_log[l], lanes_per_head)[None, :],
        dsk=jnp.repeat(ssd_d[l], lanes_per_head)[None, :], snw=ssd_norm_w[l][None, :],
        lcw=lru_conv_w[l], lcb=lru_conv_b[l][None, :], lw=lw, lb=lb,
        lam=lru_lambda[l][None, :], wout=w_out[l].astype(BF16),
        n2=norm2_w[l][None, :], wup=ffn_w_up[l].astype(BF16), fcw=ffn_conv_w[l] * half_gate,
        fcb=(ffn_conv_b[l] * half_gate[0])[None, :], wdn=ffn_w_down[l].astype(BF16))


def kernel(x, norm1_w, w_in, ssd_conv_w, ssd_conv_b, ssd_dt_bias, ssd_a_log, ssd_d, ssd_norm_w, lru_conv_w, lru_conv_b, lru_w_a, lru_b_a, lru_w_x, lru_b_x, lru_lambda, w_out, norm2_w, ffn_w_up, ffn_conv_w, ffn_conv_b, ffn_w_down, final_norm_w):
    depth = w_in.shape[0]
    seq = x.shape[1]
    assert seq % TOKENS_MIX == 0 and seq % TOKENS_FFN == 0 and x.shape[2] == D_MODEL
    tables = _retention_tables(seq)
    fin = final_norm_w[None, :]
    for l in range(depth):
        p = _layer_params(l, norm1_w, w_in, ssd_conv_w, ssd_conv_b, ssd_dt_bias, ssd_a_log, ssd_d,
                          ssd_norm_w, lru_conv_w, lru_conv_b, lru_w_a, lru_b_a, lru_w_x, lru_b_x,
                          lru_lambda, w_out, norm2_w, ffn_w_up, ffn_conv_w, ffn_conv_b, ffn_w_down)
        x = _mixer_call(x, tables, p)
        x = _ffn_call(x, p, fin, final_norm=(l == depth - 1))
    return x
```
